```python
import jax, jax.numpy as jnp
from jax import lax
import numpy as np


D_MODEL = 4096
BATCH = 2
SEQ = 8192
DEPTH = 2

CHUNK = 64
Q_BLOCK = 128

FOX_HEADS = 16
FOX_HEAD_DIM = 128
FOX_WIDTH = FOX_HEADS * FOX_HEAD_DIM

LRU_WIDTH = 2048
LRU_BLOCKS = 16
LRU_BLOCK_DIM = LRU_WIDTH // LRU_BLOCKS
LRU_C = 8.0
CONV_WIDTH = 4

MLSTM_HEADS = 8
MLSTM_QK_DIM = 128
MLSTM_V_DIM = 256
MLSTM_QK_WIDTH = MLSTM_HEADS * MLSTM_QK_DIM
MLSTM_WIDTH = MLSTM_HEADS * MLSTM_V_DIM

N_BRANCH = 3
BRANCH_WIDTH = 2048

IN_SPLITS = (FOX_WIDTH, FOX_WIDTH, FOX_WIDTH, FOX_HEADS, LRU_WIDTH, LRU_WIDTH, 2 * MLSTM_QK_WIDTH, MLSTM_WIDTH, MLSTM_WIDTH, MLSTM_HEADS, MLSTM_HEADS)
D_IN = sum(IN_SPLITS)

PEER_HEADS = 8
PEER_KEYS = 128
PEER_EXPERTS = PEER_KEYS * PEER_KEYS
PEER_QUERY_DIM = 256
PEER_HALF = PEER_QUERY_DIM // 2
PEER_TOPK = 16
PEER_TOKEN_BLOCK = 64

ALPHA = (2.0 * DEPTH) ** 0.25
BETA = (8.0 * DEPTH) ** -0.25
LN_EPS = 1e-5

kernel_name = 'hybrid_fox_rglru_mlstm_peer_deepnorm'


def _layernorm(x, g, b):
    xf = x.astype(jnp.float32)
    mu = jnp.mean(xf, axis=-1, keepdims=True)
    var = jnp.mean(jnp.square(xf - mu), axis=-1, keepdims=True)
    return ((xf - mu) * lax.rsqrt(var + LN_EPS) * g + b).astype(x.dtype)


def _split_columns(proj):
    parts, start = [], 0
    for width in IN_SPLITS:
        parts.append(proj[..., start:start + width])
        start += width
    return parts


def _causal_dwconv(x, w, b):
    c = x.shape[-1]
    y = lax.conv_general_dilated(x, w[:, None, :].astype(x.dtype), window_strides=(1,), padding=[(CONV_WIDTH - 1, 0)], dimension_numbers=('NWC', 'WIO', 'NWC'), feature_group_count=c)
    return y + b


def _fox_attention(q, k, v, f_pre):
    B, S = q.shape[0], q.shape[1]
    scale = FOX_HEAD_DIM ** -0.5
    q = jnp.transpose(q, (0, 2, 1, 3))
    k = jnp.transpose(k, (0, 2, 1, 3))
    v = jnp.transpose(v, (0, 2, 1, 3))
    log_f = jax.nn.log_sigmoid(f_pre.astype(jnp.float32))
    cum = jnp.cumsum(jnp.transpose(log_f, (0, 2, 1)), axis=-1)
    outs = []
    for i in range(S // Q_BLOCK):
        t0, t1 = i * Q_BLOCK, (i + 1) * Q_BLOCK
        logits = jnp.einsum('bhtd,bhsd->bhts', q[:, :, t0:t1], k[:, :, :t1]).astype(jnp.float32) * scale
        logits = logits + cum[:, :, t0:t1, None] - cum[:, :, None, :t1]
        causal = jnp.arange(t1)[None, :] <= jnp.arange(t0, t1)[:, None]
        p = jax.nn.softmax(jnp.where(causal, logits, -jnp.inf), axis=-1).astype(v.dtype)
        outs.append(jnp.einsum('bhts,bhsd->bhtd', p, v[:, :, :t1]))
    o = jnp.concatenate(outs, axis=2)
    return jnp.transpose(o, (0, 2, 1, 3)).reshape(B, S, FOX_WIDTH)


def _rglru(x, w_a, b_a, w_x, b_x, lam):
    B, S = x.shape[0], x.shape[1]
    xb = x.reshape(B, S, LRU_BLOCKS, LRU_BLOCK_DIM)
    r = jax.nn.sigmoid((jnp.einsum('bsnc,ncd->bsnd', xb, w_a).reshape(B, S, LRU_WIDTH) + b_a).astype(jnp.float32))
    gi = jax.nn.sigmoid((jnp.einsum('bsnc,ncd->bsnd', xb, w_x).reshape(B, S, LRU_WIDTH) + b_x).astype(jnp.float32))
    log_a = -LRU_C * r * jax.nn.softplus(-lam.astype(jnp.float32))
    a = jnp.exp(log_a)
    u = jnp.sqrt(-jnp.expm1(2.0 * log_a)) * gi * x.astype(jnp.float32)

    def combine(left, right):
        a1, b1 = left
        a2, b2 = right
        return a1 * a2, a2 * b1 + b2

    _, h = lax.associative_scan(combine, (a, u), axis=1)
    return h.astype(x.dtype)


def _mlstm(q, k, v, i_pre, f_pre):
    B, S = q.shape[0], q.shape[1]
    nc = S // CHUNK

    def to_chunks(t):
        t = t.astype(jnp.float32).reshape((B, nc, CHUNK) + t.shape[2:])
        return jnp.moveaxis(t, (1, 3), (0, 2))

    qc = to_chunks(q)
    kc = to_chunks(k) * (MLSTM_QK_DIM ** -0.5)
    vc = to_chunks(v)
    ic = to_chunks(i_pre)
    fc = jax.nn.log_sigmoid(to_chunks(f_pre))
    tril = jnp.tril(jnp.ones((CHUNK, CHUNK), dtype=bool))

    def step(carry, inp):
        C, n, m = carry
        q_c, k_c, v_c, ig, lf = inp
        g = jnp.cumsum(lf, axis=-1)
        log_d = jnp.where(tril, g[..., :, None] - g[..., None, :] + ig[..., None, :], -jnp.inf)
        m_inter = g + m[..., None]
        m_row = jnp.maximum(m_inter, jnp.max(log_d, axis=-1))
        s = jnp.einsum('bhtk,bhsk->bhts', q_c, k_c) * jnp.exp(log_d - m_row[..., None])
        inter = jnp.exp(m_inter - m_row)
        num = jnp.einsum('bhts,bhsv->bhtv', s, v_c) + inter[..., None] * jnp.einsum('bhtk,bhkv->bhtv', q_c, C)
        den = jnp.sum(s, axis=-1) + inter * jnp.einsum('bhtk,bhk->bht', q_c, n)
        h = num / jnp.maximum(jnp.abs(den), jnp.exp(-m_row))[..., None]
        g_last = g[..., -1]
        log_w = g_last[..., None] - g + ig
        m_new = jnp.maximum(g_last + m, jnp.max(log_w, axis=-1))
        w = jnp.exp(log_w - m_new[..., None])
        decay = jnp.exp(g_last + m - m_new)
        C = decay[..., None, None] * C + jnp.einsum('bhs,bhsk,bhsv->bhkv', w, k_c, v_c)
        n = decay[..., None] * n + jnp.einsum('bhs,bhsk->bhk', w, k_c)
        return (C, n, m_new), h

    init = (jnp.zeros((B, MLSTM_HEADS, MLSTM_QK_DIM, MLSTM_V_DIM), jnp.float32),
            jnp.zeros((B, MLSTM_HEADS, MLSTM_QK_DIM), jnp.float32),
            jnp.zeros((B, MLSTM_HEADS), jnp.float32))
    _, h = lax.scan(step, init, (qc, kc, vc, ic, fc))
    return jnp.moveaxis(h, (0, 2), (1, 3)).reshape(B, S, MLSTM_WIDTH)


def _mixer_block(x, w_in, fox_f_bias, lru_conv_w, lru_conv_b, lru_gate_a_w, lru_gate_a_b, lru_gate_x_w, lru_gate_x_b, lru_lambda, mlstm_conv_w, mlstm_conv_b, mlstm_i_bias, mlstm_f_bias, w_branch, w_merge_gate, b_merge_gate, w_out):
    B, S = x.shape[0], x.shape[1]
    proj = x @ w_in
    fq, fk, fv, ff, lx, lg, mqk, mv, mo, mi, mf = _split_columns(proj)
    y_fox = _fox_attention(fq.reshape(B, S, FOX_HEADS, FOX_HEAD_DIM), fk.reshape(B, S, FOX_HEADS, FOX_HEAD_DIM), fv.reshape(B, S, FOX_HEADS, FOX_HEAD_DIM), ff + fox_f_bias)
    xl = _causal_dwconv(lx, lru_conv_w, lru_conv_b)
    y_lru = _rglru(xl, lru_gate_a_w, lru_gate_a_b, lru_gate_x_w, lru_gate_x_b, lru_lambda) * jax.nn.gelu(lg)
    qk = jax.nn.silu(_causal_dwconv(mqk, mlstm_conv_w, mlstm_conv_b))
    mq, mk = qk[..., :MLSTM_QK_WIDTH], qk[..., MLSTM_QK_WIDTH:]
    h = _mlstm(mq.reshape(B, S, MLSTM_HEADS, MLSTM_QK_DIM), mk.reshape(B, S, MLSTM_HEADS, MLSTM_QK_DIM), mv.reshape(B, S, MLSTM_HEADS, MLSTM_V_DIM), mi + mlstm_i_bias, mf + mlstm_f_bias)
    y_mlstm = jax.nn.sigmoid(mo) * h.astype(x.dtype)
    merged = jax.nn.sigmoid(x @ w_merge_gate[0] + b_merge_gate[0]) * (y_fox @ w_branch[0])
    merged = merged + jax.nn.sigmoid(x @ w_merge_gate[1] + b_merge_gate[1]) * (y_lru @ w_branch[1])
    merged = merged + jax.nn.sigmoid(x @ w_merge_gate[2] + b_merge_gate[2]) * (y_mlstm @ w_branch[2])
    return merged @ w_out


def _peer(x, w_query, sub_keys, expert_u, expert_v):
    B, S, D = x.shape
    T = B * S
    xt = x.reshape(T, D)
    q = (xt @ w_query).reshape(T, PEER_HEADS, 2, PEER_HALF)
    scores = jnp.einsum('thpd,pnd->thpn', q, sub_keys).astype(jnp.float32)
    s1, i1 = lax.top_k(scores[:, :, 0], PEER_TOPK)
    s2, i2 = lax.top_k(scores[:, :, 1], PEER_TOPK)
    cand_s = (s1[..., :, None] + s2[..., None, :]).reshape(T, PEER_HEADS, PEER_TOPK * PEER_TOPK)
    cand_i = (i1[..., :, None] * PEER_KEYS + i2[..., None, :]).reshape(T, PEER_HEADS, PEER_TOPK * PEER_TOPK)
    top_s, pos = lax.top_k(cand_s, PEER_TOPK)
    experts = jnp.take_along_axis(cand_i, pos, axis=-1)
    gates = jax.nn.softmax(top_s, axis=-1)
    n_blk = T // PEER_TOKEN_BLOCK

    def block(args):
        xb, eb, gb = args
        u = jnp.take(expert_u, eb, axis=0)
        act = jax.nn.gelu(jnp.einsum('bd,bed->be', xb, u))
        vsel = jnp.take(expert_v, eb, axis=0)
        return jnp.einsum('be,bed->bd', (gb * act).astype(x.dtype), vsel)

    out = lax.map(block, (xt.reshape(n_blk, PEER_TOKEN_BLOCK, D), experts.reshape(n_blk, PEER_TOKEN_BLOCK, PEER_HEADS * PEER_TOPK), gates.reshape(n_blk, PEER_TOKEN_BLOCK, PEER_HEADS * PEER_TOPK)))
    return out.reshape(B, S, D)


def setup_inputs(seed: int = 0) -> dict:
    key = jax.random.key(seed)
    ks = jax.random.split(key, 26)
    nrm = jax.random.normal
    f32 = jnp.float32
    a0 = jax.random.uniform(ks[10], (DEPTH, LRU_WIDTH), f32, minval=0.9, maxval=0.999)
    p0 = a0 ** (1.0 / LRU_C)
    return {
        'x': nrm(ks[0], (BATCH, SEQ, D_MODEL), f32),
        'w_in': nrm(ks[1], (DEPTH, D_MODEL, D_IN), f32) * D_MODEL ** -0.5,
        'fox_f_bias': jax.random.uniform(ks[2], (DEPTH, FOX_HEADS), f32, minval=0.0, maxval=4.0),
        'lru_conv_w': nrm(ks[3], (DEPTH, CONV_WIDTH, LRU_WIDTH), f32) * CONV_WIDTH ** -0.5,
        'lru_conv_b': nrm(ks[4], (DEPTH, LRU_WIDTH), f32) * 0.01,
        'lru_gate_a_w': nrm(ks[5], (DEPTH, LRU_BLOCKS, LRU_BLOCK_DIM, LRU_BLOCK_DIM), f32) * LRU_BLOCK_DIM ** -0.5,
        'lru_gate_a_b': nrm(ks[6], (DEPTH, LRU_WIDTH), f32) * 0.01,
        'lru_gate_x_w': nrm(ks[7], (DEPTH, LRU_BLOCKS, LRU_BLOCK_DIM, LRU_BLOCK_DIM), f32) * LRU_BLOCK_DIM ** -0.5,
        'lru_gate_x_b': nrm(ks[8], (DEPTH, LRU_WIDTH), f32) * 0.01,
        'lru_lambda': jnp.log(p0) - jnp.log1p(-p0),
        'mlstm_conv_w': nrm(ks[9], (DEPTH, CONV_WIDTH, 2 * MLSTM_QK_WIDTH), f32) * CONV_WIDTH ** -0.5,
        'mlstm_conv_b': nrm(ks[11], (DEPTH, 2 * MLSTM_QK_WIDTH), f32) * 0.01,
        'mlstm_i_bias': nrm(ks[12], (DEPTH, MLSTM_HEADS), f32) * 0.1,
        'mlstm_f_bias': jax.random.uniform(ks[13], (DEPTH, MLSTM_HEADS), f32, minval=3.0, maxval=6.0),
        'w_branch': nrm(ks[14], (DEPTH, N_BRANCH, BRANCH_WIDTH, D_MODEL), f32) * BRANCH_WIDTH ** -0.5,
        'w_merge_gate': nrm(ks[15], (DEPTH, N_BRANCH, D_MODEL, D_MODEL), f32) * D_MODEL ** -0.5,
        'b_merge_gate': nrm(ks[16], (DEPTH, N_BRANCH, D_MODEL), f32) * 0.01,
        'w_out': nrm(ks[17], (DEPTH, D_MODEL, D_MODEL), f32) * (D_MODEL ** -0.5 * BETA),
        'ln1_g': 1.0 + 0.01 * nrm(ks[18], (DEPTH, D_MODEL), f32),
        'ln1_b': 0.01 * nrm(ks[19], (DEPTH, D_MODEL), f32),
        'peer_w_query': nrm(ks[20], (DEPTH, D_MODEL, PEER_HEADS * PEER_QUERY_DIM), f32) * D_MODEL ** -0.5,
        'peer_sub_keys': nrm(ks[21], (DEPTH, 2, PEER_KEYS, PEER_HALF), f32) * PEER_HALF ** -0.5,
        'peer_u': nrm(ks[22], (DEPTH, PEER_EXPERTS, D_MODEL), f32) * D_MODEL ** -0.5,
        'peer_v': nrm(ks[23], (DEPTH, PEER_EXPERTS, D_MODEL), f32) * BETA,
        'ln2_g': 1.0 + 0.01 * nrm(ks[24], (DEPTH, D_MODEL), f32),
        'ln2_b': 0.01 * nrm(ks[25], (DEPTH, D_MODEL), f32),
    }


def reference(x, w_in, fox_f_bias, lru_conv_w, lru_conv_b, lru_gate_a_w, lru_gate_a_b, lru_gate_x_w, lru_gate_x_b, lru_lambda, mlstm_conv_w, mlstm_conv_b, mlstm_i_bias, mlstm_f_bias, w_branch, w_merge_gate, b_merge_gate, w_out, ln1_g, ln1_b, peer_w_query, peer_sub_keys, peer_u, peer_v, ln2_g, ln2_b):
    h = x
    for l in range(DEPTH):
        mix = _mixer_block(h, w_in[l], fox_f_bias[l], lru_conv_w[l], lru_conv_b[l], lru_gate_a_w[l], lru_gate_a_b[l], lru_gate_x_w[l], lru_gate_x_b[l], lru_lambda[l], mlstm_conv_w[l], mlstm_conv_b[l], mlstm_i_bias[l], mlstm_f_bias[l], w_branch[l], w_merge_gate[l], b_merge_gate[l], w_out[l])
        h = _layernorm(ALPHA * h + mix, ln1_g[l], ln1_b[l])
        ffn = _peer(h, peer_w_query[l], peer_sub_keys[l], peer_u[l], peer_v[l])
        h = _layernorm(ALPHA * h + ffn, ln2_g[l], ln2_b[l])
    return h
```

```python
import functools
import math

import jax
import jax.numpy as jnp
from jax import lax
from jax.experimental import pallas as pl
from jax.experimental.pallas import tpu as pltpu

F32 = jnp.float32
BF16 = jnp.bfloat16

HEAD_DIM = 128
MLSTM_V_DIM = 256
CHUNK = 64
CONV_WIDTH = 4
LRU_C = 8.0
PEER_KEYS = 128
PEER_HALF = 128
PEER_TOPK = 16
LN_EPS = 1e-5

V7X_VMEM_BYTES = 64 * 1024 * 1024
VMEM_LIMIT = V7X_VMEM_BYTES - 8 * 1024 * 1024
LANES = 128
SUBLANES = 8

NEG_INF = float("-inf")


def _cparams(*sem):
    return pltpu.CompilerParams(dimension_semantics=sem, vmem_limit_bytes=VMEM_LIMIT)


def _log_sigmoid(x):
    return jnp.minimum(x, 0.0) - jnp.log1p(jnp.exp(-jnp.abs(x)))


def _sigmoid(x):
    return 1.0 / (1.0 + jnp.exp(-x))


def _gelu_tanh(x):
    c = math.sqrt(2.0 / math.pi)
    return 0.5 * x * (1.0 + jnp.tanh(c * (x + 0.044715 * (x * x * x))))


def _expm1(z):
    u = jnp.exp(z)
    um1 = u - 1.0
    near = um1 * z / jnp.log(u)
    return jnp.where(jnp.abs(z) > 0.5, um1, jnp.where(u == 1.0, z, near))


def _split3(x):
    hi = x.astype(BF16)
    r1 = x - hi.astype(F32)
    mid = r1.astype(BF16)
    lo = (r1 - mid.astype(F32)).astype(BF16)
    return hi, mid, lo


def _dot(a, b):
    return jnp.dot(a, b, preferred_element_type=F32)


def _dot_nt(a, b):
    return lax.dot_general(a, b, (((1,), (1,)), ((), ())), preferred_element_type=F32)


def _dot_tn(a, b):
    return lax.dot_general(a, b, (((0,), (0,)), ((), ())), preferred_element_type=F32)


def _dot3_left(tri, x):
    hi, mid, lo = _split3(x)
    return _dot(tri, hi) + _dot(tri, mid) + _dot(tri, lo)


def _dot3_right(x, tri):
    hi, mid, lo = _split3(x)
    return _dot(hi, tri) + _dot(mid, tri) + _dot(lo, tri)


def _mm_kernel(a_ref, b_ref, o_ref):
    o_ref[...] = _dot(a_ref[...], b_ref[...]).astype(o_ref.dtype)


def _matmul(a, b, out_dtype, *, tm, tn, name):
    m, k = a.shape
    n = b.shape[1]
    tm, tn = min(tm, m), min(tn, n)
    return pl.pallas_call(
        _mm_kernel,
        out_shape=jax.ShapeDtypeStruct((m, n), out_dtype),
        grid=(m // tm, n // tn),
        in_specs=[pl.BlockSpec((tm, k), lambda i, j: (i, 0)),
                  pl.BlockSpec((k, tn), lambda i, j: (0, j))],
        out_specs=pl.BlockSpec((tm, tn), lambda i, j: (i, j)),
        compiler_params=_cparams("parallel", "parallel"),
        name=name,
    )(a, b)


def _gate_kernel(z_ref, b_ref, o_ref, carry_ref, *, fh, mh, ts):
    @pl.when(pl.program_id(1) == 0)
    def _():
        carry_ref[...] = jnp.zeros_like(carry_ref)

    z = z_ref[...] + b_ref[...]
    ls = _log_sigmoid(z)
    lane = lax.broadcasted_iota(jnp.int32, z.shape, 1)
    row = lax.broadcasted_iota(jnp.int32, (ts, ts), 0)
    col = lax.broadcasted_iota(jnp.int32, (ts, ts), 1)
    tri = jnp.where(col <= row, 1.0, 0.0).astype(BF16)
    cum = _dot3_left(tri, ls) + carry_ref[...]
    carry_ref[...] = cum[ts - 1:ts, :]
    o_ref[...] = jnp.where(lane < fh, cum, jnp.where(lane < fh + mh, z, ls))


def _gate_prep(z, bias, *, batch, fh, mh):
    t = z.shape[0]
    s = t // batch
    ts = min(256, s)
    ns = s // ts
    return pl.pallas_call(
        functools.partial(_gate_kernel, fh=fh, mh=mh, ts=ts),
        out_shape=jax.ShapeDtypeStruct((t, LANES), F32),
        grid=(batch, ns),
        in_specs=[pl.BlockSpec((ts, LANES), lambda b, i: (b * ns + i, 0)),
                  pl.BlockSpec((1, LANES), lambda b, i: (0, 0))],
        out_specs=pl.BlockSpec((ts, LANES), lambda b, i: (b * ns + i, 0)),
        scratch_shapes=[pltpu.VMEM((1, LANES), F32)],
        compiler_params=_cparams("parallel", "arbitrary"),
        name="gate_prep",
    )(z, bias)


def _fox_kernel(q_ref, k_ref, v_ref, cc_ref, cr_ref, o_ref, m_ref, l_ref, acc_ref, *, tq, scale):
    qi = pl.program_id(2)
    q = q_ref[...]
    cq = cc_ref[...]
    m_ref[...] = jnp.full_like(m_ref, NEG_INF)
    l_ref[...] = jnp.zeros_like(l_ref)
    acc_ref[...] = jnp.zeros_like(acc_ref)

    def block(kj, masked):
        start = pl.multiple_of(kj * tq, tq)
        k = k_ref[pl.ds(start, tq), :]
        v = v_ref[pl.ds(start, tq), :]
        ck = cr_ref[:, pl.ds(start, tq)]
        s = _dot_nt(q, k) * scale + (cq - ck)
        if masked:
            row = lax.broadcasted_iota(jnp.int32, (tq, tq), 0)
            col = lax.broadcasted_iota(jnp.int32, (tq, tq), 1)
            s = jnp.where(col <= row, s, NEG_INF)
        m_prev = m_ref[...]
        m_new = jnp.maximum(m_prev, jnp.max(s, axis=1, keepdims=True))
        p = jnp.exp(s - m_new)
        alpha = jnp.exp(m_prev - m_new)
        l_ref[...] = alpha * l_ref[...] + jnp.sum(p, axis=1, keepdims=True)
        acc_ref[...] = alpha * acc_ref[...] + _dot(p.astype(BF16), v)
        m_ref[...] = m_new

    def body(kj, c):
        block(kj, False)
        return c

    lax.fori_loop(0, qi, body, 0)
    block(qi, True)
    o_ref[...] = (acc_ref[...] / l_ref[...]).astype(o_ref.dtype)


def _fox_attention(proj, cum_col, cum_row, *, batch, fh, tq):
    t = proj.shape[0]
    s = t // batch
    tq = min(tq, s)
    nq = s // tq
    kern = functools.partial(_fox_kernel, tq=tq, scale=HEAD_DIM ** -0.5)
    return pl.pallas_call(
        kern,
        out_shape=jax.ShapeDtypeStruct((t, fh * HEAD_DIM), BF16),
        grid=(batch, fh, nq),
        in_specs=[pl.BlockSpec((tq, HEAD_DIM), lambda b, h, i: (b * nq + i, h)),
                  pl.BlockSpec((s, HEAD_DIM), lambda b, h, i: (b, fh + h)),
                  pl.BlockSpec((s, HEAD_DIM), lambda b, h, i: (b, 2 * fh + h)),
                  pl.BlockSpec((None, None, tq, 1), lambda b, h, i: (b, h, i, 0)),
                  pl.BlockSpec((None, None, 1, s), lambda b, h, i: (b, h, 0, 0))],
        out_specs=pl.BlockSpec((tq, HEAD_DIM), lambda b, h, i: (b * nq + i, h)),
        scratch_shapes=[pltpu.VMEM((tq, 1), F32), pltpu.VMEM((tq, 1), F32),
                        pltpu.VMEM((tq, HEAD_DIM), F32)],
        compiler_params=_cparams("parallel", "parallel", "arbitrary"),
        name="fox_attention",
    )(proj, proj, proj, cum_col, cum_row)


def _causal_conv(x, ext_ref, w_ref, b_ref, ts, first):
    @pl.when(first)
    def _():
        ext_ref[0:SUBLANES, :] = jnp.zeros((SUBLANES, x.shape[1]), F32)

    ext_ref[SUBLANES:SUBLANES + ts, :] = x
    y = b_ref[...] + w_ref[CONV_WIDTH - 1:CONV_WIDTH, :] * x
    for k in range(CONV_WIDTH - 1):
        off = SUBLANES - (CONV_WIDTH - 1) + k
        y = y + w_ref[k:k + 1, :] * ext_ref[off:off + ts, :]
    ext_ref[0:SUBLANES, :] = ext_ref[ts:ts + SUBLANES, :]
    return y


def _lru_kernel(lx_ref, lg_ref, cw_ref, cb_ref, wa_ref, ba_ref, wx_ref, bx_ref, lam_ref, o_ref,
                ext_ref, a_ref, u_ref, h_ref, *, ts, nblk):
    first = pl.program_id(1) == 0

    @pl.when(first)
    def _():
        h_ref[...] = jnp.zeros_like(h_ref)

    xl = _causal_conv(lx_ref[...].astype(F32), ext_ref, cw_ref, cb_ref, ts, first)
    lam = lam_ref[...]
    nsp = jnp.maximum(-lam, 0.0) + jnp.log1p(jnp.exp(-jnp.abs(lam)))
    for n in range(nblk):
        sl = slice(n * HEAD_DIM, (n + 1) * HEAD_DIM)
        xb = xl[:, sl]
        xb16 = xb.astype(BF16)
        r = _sigmoid(_dot(xb16, wa_ref[n]) + ba_ref[:, sl])
        gi = _sigmoid(_dot(xb16, wx_ref[n]) + bx_ref[:, sl])
        log_a = (-LRU_C) * r * nsp[:, sl]
        a_ref[:, sl] = jnp.exp(log_a)
        u_ref[:, sl] = jnp.sqrt(-_expm1(2.0 * log_a)) * gi * xb

    def group(g, h):
        base = pl.multiple_of(g * SUBLANES, SUBLANES)
        a8 = a_ref[pl.ds(base, SUBLANES), :]
        u8 = u_ref[pl.ds(base, SUBLANES), :]
        rows = []
        for r in range(SUBLANES):
            h = a8[r:r + 1, :] * h + u8[r:r + 1, :]
            rows.append(h)
        u_ref[pl.ds(base, SUBLANES), :] = jnp.concatenate(rows, axis=0)
        return h

    h_ref[...] = lax.fori_loop(0, ts // SUBLANES, group, h_ref[...])
    o_ref[...] = (u_ref[...] * _gelu_tanh(lg_ref[...].astype(F32))).astype(o_ref.dtype)


def _lru_branch(proj, conv_w, conv_b, w_a, b_a, w_x, b_x, lam, *, batch, bw, ts):
    t = proj.shape[0]
    s = t // batch
    ts = min(ts, s)
    ns = s // ts
    nblk = bw // HEAD_DIM
    row = lambda b, i: b * ns + i
    full2 = lambda b, i: (0, 0)
    full3 = lambda b, i: (0, 0, 0)
    return pl.pallas_call(
        functools.partial(_lru_kernel, ts=ts, nblk=nblk),
        out_shape=jax.ShapeDtypeStruct((t, bw), BF16),
        grid=(batch, ns),
        in_specs=[pl.BlockSpec((ts, bw), lambda b, i: (row(b, i), 3)),
                  pl.BlockSpec((ts, bw), lambda b, i: (row(b, i), 4)),
                  pl.BlockSpec((CONV_WIDTH, bw), full2),
                  pl.BlockSpec((1, bw), full2),
                  pl.BlockSpec((nblk, HEAD_DIM, HEAD_DIM), full3),
                  pl.BlockSpec((1, bw), full2),
                  pl.BlockSpec((nblk, HEAD_DIM, HEAD_DIM), full3),
                  pl.BlockSpec((1, bw), full2),
                  pl.BlockSpec((1, bw), full2)],
        out_specs=pl.BlockSpec((ts, bw), lambda b, i: (row(b, i), 0)),
        scratch_shapes=[pltpu.VMEM((ts + SUBLANES, bw), F32), pltpu.VMEM((ts, bw), F32),
                        pltpu.VMEM((ts, bw), F32), pltpu.VMEM((1, bw), F32)],
        compiler_params=_cparams("parallel", "arbitrary"),
        name="rglru_branch",
    )(proj, proj, conv_w, conv_b, w_a, b_a, w_x, b_x, lam)


def _mlstm_kernel(qk_ref, v_ref, og_ref, g_ref, gt_ref, cw_ref, cb_ref, o_ref,
                  ext_ref, qk_s, c_ref, n_ref, m_ref, *, ts, mh, gate_lane):
    first = pl.program_id(1) == 0

    @pl.when(first)
    def _():
        c_ref[...] = jnp.zeros_like(c_ref)
        n_ref[...] = jnp.zeros_like(n_ref)
        m_ref[...] = jnp.zeros_like(m_ref)

    conv = _causal_conv(qk_ref[...].astype(F32), ext_ref, cw_ref, cb_ref, ts, first)
    qk_s[...] = conv * _sigmoid(conv)
    mqk = mh * HEAD_DIM
    kscale = HEAD_DIM ** -0.5

    row = lax.broadcasted_iota(jnp.int32, (CHUNK, CHUNK), 0)
    col = lax.broadcasted_iota(jnp.int32, (CHUNK, CHUNK), 1)
    tril = col <= row
    tri_lo = jnp.where(tril, 1.0, 0.0).astype(BF16)
    tri_up = jnp.where(row <= col, 1.0, 0.0).astype(BF16)

    def chunk(c, carry):
        r0 = pl.multiple_of(c * CHUNK, CHUNK)
        gates = g_ref[pl.ds(r0, CHUNK), :]
        gates_t = gt_ref[c]
        cum_c = _dot3_left(tri_lo, gates)
        cum_r = _dot3_right(gates_t, tri_up)
        for h in range(mh):
            li, lf = gate_lane + h, gate_lane + mh + h
            g_col = cum_c[:, lf:lf + 1]
            g_row = cum_r[lf:lf + 1, :]
            ig_col = gates[:, li:li + 1]
            ig_row = gates_t[li:li + 1, :]
            q = qk_s[pl.ds(r0, CHUNK), h * HEAD_DIM:(h + 1) * HEAD_DIM]
            k = qk_s[pl.ds(r0, CHUNK), mqk + h * HEAD_DIM:mqk + (h + 1) * HEAD_DIM] * kscale
            v = v_ref[pl.ds(r0, CHUNK), h * MLSTM_V_DIM:(h + 1) * MLSTM_V_DIM]
            q16 = q.astype(BF16)
            m_prev = m_ref[h]
            c_prev = c_ref[h]
            n_prev = n_ref[h]

            log_d = jnp.where(tril, g_col - g_row + ig_row, NEG_INF)
            m_inter = g_col + m_prev
            m_row = jnp.maximum(m_inter, jnp.max(log_d, axis=1, keepdims=True))
            sc = _dot_nt(q16, k.astype(BF16)) * jnp.exp(log_d - m_row)
            inter = jnp.exp(m_inter - m_row)
            num = _dot(sc.astype(BF16), v) + inter * _dot(q16, c_prev.astype(BF16))
            den = jnp.sum(sc, axis=1, keepdims=True) + inter * jnp.sum(q * n_prev, axis=1, keepdims=True)
            hh = num / jnp.maximum(jnp.abs(den), jnp.exp(-m_row))
            og = og_ref[pl.ds(r0, CHUNK), h * MLSTM_V_DIM:(h + 1) * MLSTM_V_DIM].astype(F32)
            o_ref[pl.ds(r0, CHUNK), h * MLSTM_V_DIM:(h + 1) * MLSTM_V_DIM] = (
                _sigmoid(og) * hh).astype(o_ref.dtype)

            g_last = g_col[CHUNK - 1:CHUNK, :]
            log_w = g_last - g_col + ig_col
            m_new = jnp.maximum(g_last + m_prev, jnp.max(log_w, axis=0, keepdims=True))
            w = jnp.exp(log_w - m_new)
            decay = jnp.exp(g_last + m_prev - m_new)
            wk = w * k
            c_ref[h] = decay * c_prev + _dot_tn(wk.astype(BF16), v)
            n_ref[h] = decay * n_prev + jnp.sum(wk, axis=0, keepdims=True)
            m_ref[h] = m_new
        return carry

    lax.fori_loop(0, ts // CHUNK, chunk, 0)


def _mlstm_branch(proj, gates, gates_t, conv_w, conv_b, *, batch, bw, fh, ts):
    t = proj.shape[0]
    s = t // batch
    ts = min(ts, s)
    ns = s // ts
    mh = bw // MLSTM_V_DIM
    row = lambda b, i: b * ns + i
    full2 = lambda b, i: (0, 0)
    kern = functools.partial(_mlstm_kernel, ts=ts, mh=mh, gate_lane=fh)
    return pl.pallas_call(
        kern,
        out_shape=jax.ShapeDtypeStruct((t, bw), BF16),
        grid=(batch, ns),
        in_specs=[pl.BlockSpec((ts, bw), lambda b, i: (row(b, i), 5)),
                  pl.BlockSpec((ts, bw), lambda b, i: (row(b, i), 6)),
                  pl.BlockSpec((ts, bw), lambda b, i: (row(b, i), 7)),
                  pl.BlockSpec((ts, LANES), lambda b, i: (row(b, i), 0)),
                  pl.BlockSpec((ts // CHUNK, LANES, CHUNK), lambda b, i: (row(b, i), 0, 0)),
                  pl.BlockSpec((CONV_WIDTH, bw), full2),
                  pl.BlockSpec((1, bw), full2)],
        out_specs=pl.BlockSpec((ts, bw), lambda b, i: (row(b, i), 0)),
        scratch_shapes=[pltpu.VMEM((ts + SUBLANES, bw), F32), pltpu.VMEM((ts, bw), F32),
                        pltpu.VMEM((mh, HEAD_DIM, MLSTM_V_DIM), F32),
                        pltpu.VMEM((mh, 1, HEAD_DIM), F32), pltpu.VMEM((mh, 1, 1), F32)],
        compiler_params=_cparams("parallel", "arbitrary"),
        name="mlstm_branch",
    )(proj, proj, proj, gates, gates_t, conv_w, conv_b)


def _merge_kernel(x_ref, y0_ref, y1_ref, y2_ref, wg_ref, bg_ref, wb_ref, o_ref, acc_ref):
    b = pl.program_id(2)
    gate = _sigmoid(_dot(x_ref[...], wg_ref[...]) + bg_ref[...])

    def contrib(y_ref):
        return gate * _dot(y_ref[...], wb_ref[...])

    @pl.when(b == 0)
    def _():
        acc_ref[...] = contrib(y0_ref)

    @pl.when(b == 1)
    def _():
        acc_ref[...] += contrib(y1_ref)

    @pl.when(b == 2)
    def _():
        o_ref[...] = (acc_ref[...] + contrib(y2_ref)).astype(o_ref.dtype)


def _merge(x16, ys, wg, bg, wb, *, tm, tn):
    t, d = x16.shape
    bw = ys[0].shape[1]
    tm, tn = min(tm, t), min(tn, d)
    rows = lambda i, j, b: (i, 0)
    return pl.pallas_call(
        _merge_kernel,
        out_shape=jax.ShapeDtypeStruct((t, d), BF16),
        grid=(t // tm, d // tn, 3),
        in_specs=[pl.BlockSpec((tm, d), rows),
                  pl.BlockSpec((tm, bw), rows), pl.BlockSpec((tm, bw), rows), pl.BlockSpec((tm, bw), rows),
                  pl.BlockSpec((None, d, tn), lambda i, j, b: (b, 0, j)),
                  pl.BlockSpec((None, 1, tn), lambda i, j, b: (b, 0, j)),
                  pl.BlockSpec((None, bw, tn), lambda i, j, b: (b, 0, j))],
        out_specs=pl.BlockSpec((tm, tn), lambda i, j, b: (i, j)),
        scratch_shapes=[pltpu.VMEM((tm, tn), F32)],
        compiler_params=_cparams("parallel", "parallel", "arbitrary"),
        name="merge_branches",
    )(x16, ys[0], ys[1], ys[2], wg, bg, wb)


def _ln_kernel(r_ref, d_ref, g_ref, b_ref, o_ref, o16_ref, *, alpha):
    x = alpha * r_ref[...] + d_ref[...]
    mu = jnp.mean(x, axis=-1, keepdims=True)
    xc = x - mu
    var = jnp.mean(xc * xc, axis=-1, keepdims=True)
    y = xc * lax.rsqrt(var + LN_EPS) * g_ref[...] + b_ref[...]
    o_ref[...] = y
    o16_ref[...] = y.astype(BF16)


def _post_norm(resid, delta, g, b, *, alpha, tr):
    t, d = resid.shape
    tr = min(tr, t)
    rows = lambda i: (i, 0)
    return pl.pallas_call(
        functools.partial(_ln_kernel, alpha=alpha),
        out_shape=(jax.ShapeDtypeStruct((t, d), F32), jax.ShapeDtypeStruct((t, d), BF16)),
        grid=(t // tr,),
        in_specs=[pl.BlockSpec((tr, d), rows), pl.BlockSpec((tr, d), rows),
                  pl.BlockSpec((1, d), lambda i: (0, 0)), pl.BlockSpec((1, d), lambda i: (0, 0))],
        out_specs=(pl.BlockSpec((tr, d), rows), pl.BlockSpec((tr, d), rows)),
        compiler_params=_cparams("parallel"),
        name="post_norm",
    )(resid, delta, g, b)


def _topk_rows(a, k):
    r, n = a.shape
    rid = lax.broadcasted_iota(jnp.int32, (r, n), 0)
    kid = lax.broadcasted_iota(jnp.int32, (k, n), 0)

    def step(i, carry):
        a, vals, idx = carry
        m = jnp.max(a, axis=0, keepdims=True)
        first = jnp.min(jnp.where(a == m, rid, r), axis=0, keepdims=True)
        vals = jnp.where(kid == i, m, vals)
        idx = jnp.where(kid == i, first, idx)
        a = jnp.where(rid == first, NEG_INF, a)
        return a, vals, idx

    _, vals, idx = lax.fori_loop(
        0, k, step, (a, jnp.zeros((k, n), F32), jnp.zeros((k, n), jnp.int32)))
    return vals, idx


def _take_rows(table, sel):
    k, n = table.shape
    out = jnp.zeros((k, n), jnp.int32)
    for a in range(k):
        out = jnp.where(sel == a, table[a:a + 1, :], out)
    return out


def _route_kernel(q_ref, keys_ref, i_ref, j_ref, g_ref):
    q = q_ref[...]
    s1 = _dot_nt(keys_ref[0], q[:, :PEER_HALF])
    s2 = _dot_nt(keys_ref[1], q[:, PEER_HALF:])
    v1, i1 = _topk_rows(s1, PEER_TOPK)
    v2, i2 = _topk_rows(s2, PEER_TOPK)
    n = q.shape[0]
    cand = (v1[:, None, :] + v2[None, :, :]).reshape(PEER_TOPK * PEER_TOPK, n)
    top, pos = _topk_rows(cand, PEER_TOPK)
    e = jnp.exp(top - top[0:1, :])
    g_ref[...] = e / jnp.sum(e, axis=0, keepdims=True)
    shift = PEER_TOPK.bit_length() - 1
    i_ref[...] = _take_rows(i1, jnp.right_shift(pos, shift))
    j_ref[...] = _take_rows(i2, jnp.bitwise_and(pos, PEER_TOPK - 1))


def _peer_route(q16, keys16, *, tt):
    t = q16.shape[0]
    ph = q16.shape[1] // (2 * PEER_HALF)
    tt = min(tt, t)
    n = ph * PEER_TOPK
    out = jax.ShapeDtypeStruct((n, t), jnp.int32)
    ospec = pl.BlockSpec((PEER_TOPK, tt), lambda i, h: (h, i))
    return pl.pallas_call(
        _route_kernel,
        out_shape=(out, out, jax.ShapeDtypeStruct((n, t), F32)),
        grid=(t // tt, ph),
        in_specs=[pl.BlockSpec((tt, 2 * PEER_HALF), lambda i, h: (i, h)),
                  pl.BlockSpec((2, PEER_KEYS, PEER_HALF), lambda i, h: (0, 0, 0))],
        out_specs=(ospec, ospec, ospec),
        compiler_params=_cparams("parallel", "parallel"),
        name="peer_route",
    )(q16, keys16)


def _gates_kernel(i_ref, j_ref, g_ref, o_ref, scr_ref, *, tb):
    n = i_ref.shape[1]
    sub = lax.broadcasted_iota(jnp.int32, (PEER_KEYS, n), 0)

    def token(t, c):
        irow = i_ref[pl.ds(t, 1), :]
        jrow = j_ref[pl.ds(t, 1), :]
        grow = g_ref[pl.ds(t, 1), :]
        a = jnp.where(sub == irow, grow, 0.0).astype(BF16)
        b = jnp.where(sub == jrow, 1.0, 0.0).astype(BF16)
        base = pl.multiple_of(t * PEER_KEYS, PEER_KEYS)
        scr_ref[pl.ds(base, PEER_KEYS), :] = _dot_nt(a, b)
        return c

    lax.fori_loop(0, tb, token, 0)
    for i in range(PEER_KEYS):
        o_ref[:, i * PEER_KEYS:(i + 1) * PEER_KEYS] = (
            scr_ref[pl.ds(i, tb, stride=PEER_KEYS), :].astype(o_ref.dtype))


def _peer_gates(sel_i, sel_j, gates, *, tb):
    t, n = sel_i.shape
    tb = min(tb, t)
    e = PEER_KEYS * PEER_KEYS
    rows = lambda i: (i, 0)
    return pl.pallas_call(
        functools.partial(_gates_kernel, tb=tb),
        out_shape=jax.ShapeDtypeStruct((t, e), BF16),
        grid=(t // tb,),
        in_specs=[pl.BlockSpec((tb, n), rows), pl.BlockSpec((tb, n), rows), pl.BlockSpec((tb, n), rows)],
        out_specs=pl.BlockSpec((tb, e), rows),
        scratch_shapes=[pltpu.VMEM((tb * PEER_KEYS, PEER_KEYS), F32)],
        compiler_params=_cparams("parallel"),
        name="peer_gates",
    )(sel_i, sel_j, gates)


def _expert_kernel(x_ref, ut_ref, v_ref, w_ref, o_ref):
    act = _gelu_tanh(_dot(x_ref[...], ut_ref[...]))
    p = (w_ref[...].astype(F32) * act).astype(BF16)
    contrib = _dot(p, v_ref[...])

    @pl.when(pl.program_id(1) == 0)
    def _():
        o_ref[...] = contrib

    @pl.when(pl.program_id(1) != 0)
    def _():
        o_ref[...] += contrib


def _peer_experts(x16, ut16, v16, w16, *, tt, te):
    t, d = x16.shape
    e = ut16.shape[1]
    tt, te = min(tt, t), min(te, e)
    return pl.pallas_call(
        _expert_kernel,
        out_shape=jax.ShapeDtypeStruct((t, d), F32),
        grid=(t // tt, e // te),
        in_specs=[pl.BlockSpec((tt, d), lambda i, j: (i, 0)),
                  pl.BlockSpec((d, te), lambda i, j: (0, j)),
                  pl.BlockSpec((te, d), lambda i, j: (j, 0)),
                  pl.BlockSpec((tt, te), lambda i, j: (i, j))],
        out_specs=pl.BlockSpec((tt, d), lambda i, j: (i, 0)),
        compiler_params=_cparams("parallel", "arbitrary"),
        name="peer_experts",
    )(x16, ut16, v16, w16)


def _mixer(h16, batch, w_in, fox_f_bias, lru_conv_w, lru_conv_b, lru_gate_a_w, lru_gate_a_b,
           lru_gate_x_w, lru_gate_x_b, lru_lambda, mlstm_conv_w, mlstm_conv_b, mlstm_i_bias,
           mlstm_f_bias, w_branch, w_merge_gate, b_merge_gate, w_out):
    t, d = h16.shape
    bw = w_branch.shape[1]
    fh = bw // HEAD_DIM
    mh = bw // MLSTM_V_DIM
    s = t // batch

    o_ff = 3 * bw
    o_lx = o_ff + fh
    o_mi = o_lx + 5 * bw
    w_main = jnp.concatenate([w_in[:, :o_ff], w_in[:, o_lx:o_mi]], axis=1).astype(BF16)
    n_gate = fh + 2 * mh
    w_gate = jnp.concatenate([w_in[:, o_ff:o_lx], w_in[:, o_mi:]], axis=1)
    w_gate = jnp.pad(w_gate, ((0, 0), (0, LANES - n_gate))).astype(BF16)
    gate_bias = jnp.pad(jnp.concatenate([fox_f_bias, mlstm_i_bias, mlstm_f_bias]),
                        (0, LANES - n_gate)).reshape(1, LANES)

    proj = _matmul(h16, w_main, BF16, tm=1024, tn=1024, name="in_proj")
    gate_z = _matmul(h16, w_gate, F32, tm=1024, tn=LANES, name="gate_proj")
    gates = _gate_prep(gate_z, gate_bias, batch=batch, fh=fh, mh=mh)
    gates_t = gates.T
    cum = gates_t[:fh].reshape(fh, batch, s).transpose(1, 0, 2)

    y_fox = _fox_attention(proj, cum[..., None], cum[:, :, None, :], batch=batch, fh=fh, tq=512)
    y_lru = _lru_branch(proj, lru_conv_w, lru_conv_b.reshape(1, bw), lru_gate_a_w.astype(BF16),
                        lru_gate_a_b.reshape(1, bw), lru_gate_x_w.astype(BF16),
                        lru_gate_x_b.reshape(1, bw), lru_lambda.reshape(1, bw),
                        batch=batch, bw=bw, ts=256)
    gates_ct = gates.reshape(t // CHUNK, CHUNK, LANES).transpose(0, 2, 1)
    y_mlstm = _mlstm_branch(proj, gates, gates_ct, mlstm_conv_w, mlstm_conv_b.reshape(1, bw),
                            batch=batch, bw=bw, fh=fh, ts=512)
    merged = _merge(h16, (y_fox, y_lru, y_mlstm), w_merge_gate.astype(BF16),
                    b_merge_gate.reshape(3, 1, d), w_branch.astype(BF16), tm=512, tn=512)
    return _matmul(merged, w_out.astype(BF16), F32, tm=1024, tn=1024, name="out_proj")


def _peer(h16, w_query, sub_keys, expert_u, expert_v):
    q16 = _matmul(h16, w_query.astype(BF16), BF16, tm=1024, tn=1024, name="peer_query")
    sel_i, sel_j, gates = _peer_route(q16, sub_keys.astype(BF16), tt=256)
    w16 = _peer_gates(sel_i.T, sel_j.T, gates.T, tb=128)
    return _peer_experts(h16, expert_u.T.astype(BF16), expert_v.astype(BF16), w16, tt=512, te=512)


def kernel(x, w_in, fox_f_bias, lru_conv_w, lru_conv_b, lru_gate_a_w, lru_gate_a_b, lru_gate_x_w, lru_gate_x_b, lru_lambda, mlstm_conv_w, mlstm_conv_b, mlstm_i_bias, mlstm_f_bias, w_branch, w_merge_gate, b_merge_gate, w_out, ln1_g, ln1_b, peer_w_query, peer_sub_keys, peer_u, peer_v, ln2_g, ln2_b):
    batch, seq, d = x.shape
    depth = w_in.shape[0]
    alpha = (2.0 * depth) ** 0.25
    h = x.reshape(batch * seq, d)
    h16 = h.astype(BF16)
    for l in range(depth):
        mix = _mixer(h16, batch, w_in[l], fox_f_bias[l], lru_conv_w[l], lru_conv_b[l], lru_gate_a_w[l],
                     lru_gate_a_b[l], lru_gate_x_w[l], lru_gate_x_b[l], lru_lambda[l], mlstm_conv_w[l],
                     mlstm_conv_b[l], mlstm_i_bias[l], mlstm_f_bias[l], w_branch[l], w_merge_gate[l],
                     b_merge_gate[l], w_out[l])
        h, h16 = _post_norm(h, mix, ln1_g[l].reshape(1, d), ln1_b[l].reshape(1, d), alpha=alpha, tr=256)
        ffn = _peer(h16, peer_w_query[l], peer_sub_keys[l], peer_u[l], peer_v[l])
        h, h16 = _post_norm(h, ffn, ln2_g[l].reshape(1, d), ln2_b[l].reshape(1, d), alpha=alpha, tr=256)
    return h.reshape(batch, seq, d)
```

```python
import functools
import math

import jax
import jax.numpy as jnp
from jax import lax
from jax.experimental import pallas as pl
from jax.experimental.pallas import tpu as pltpu

F32 = jnp.float32
BF16 = jnp.bfloat16

HEAD_DIM = 128
MLSTM_V_DIM = 256
CHUNK = 64
CONV_WIDTH = 4
LRU_C = 8.0
PEER_KEYS = 128
PEER_HALF = 128
PEER_TOPK = 16
LN_EPS = 1e-5
GATE_TOKEN_UNROLL = 8

V7X_VMEM_BYTES = 64 * 1024 * 1024
VMEM_LIMIT = V7X_VMEM_BYTES - 8 * 1024 * 1024
LANES = 128
SUBLANES = 8

NEG_INF = float("-inf")
LOG2E = math.log2(math.e)

PROJ_FK, PROJ_LX, PROJ_LG, PROJ_MQK, PROJ_MV, PROJ_MO = range(6)


def _tile(dim, pref, align):
    if dim <= pref:
        return dim
    t = pref - pref % align
    while dim % t:
        t -= align
    assert t > 0, (dim, pref, align)
    return t


def _resident_spec(block_shape, index_map):
    return pl.BlockSpec(block_shape, index_map, pipeline_mode=pl.Buffered(1))


def _cparams(*sem):
    return pltpu.CompilerParams(dimension_semantics=sem, vmem_limit_bytes=VMEM_LIMIT)


def _log_sigmoid(x):
    return jnp.minimum(x, 0.0) - jnp.log1p(jnp.exp(-jnp.abs(x)))


def _sigmoid(x):
    return 1.0 / (1.0 + jnp.exp(-x))


def _gelu_tanh(x):
    c = math.sqrt(2.0 / math.pi)
    return 0.5 * x * (1.0 + jnp.tanh(c * (x + 0.044715 * (x * x * x))))


def _expm1(z):
    u = jnp.exp(z)
    um1 = u - 1.0
    near = um1 * z / jnp.log(u)
    return jnp.where(jnp.abs(z) > 0.5, um1, jnp.where(u == 1.0, z, near))


def _split3(x):
    hi = x.astype(BF16)
    r1 = x - hi.astype(F32)
    mid = r1.astype(BF16)
    lo = (r1 - mid.astype(F32)).astype(BF16)
    return hi, mid, lo


def _dot(a, b):
    return jnp.dot(a, b, preferred_element_type=F32)


def _dot_nt(a, b):
    return lax.dot_general(a, b, (((1,), (1,)), ((), ())), preferred_element_type=F32)


def _dot_tn(a, b):
    return lax.dot_general(a, b, (((0,), (0,)), ((), ())), preferred_element_type=F32)


def _dot3_left(tri, x):
    hi, mid, lo = _split3(x)
    return _dot(tri, hi) + _dot(tri, mid) + _dot(tri, lo)


def _dot3_right(x, tri):
    hi, mid, lo = _split3(x)
    return _dot(hi, tri) + _dot(mid, tri) + _dot(lo, tri)


def _mm_kernel(a_ref, b_ref, o_ref):
    o_ref[...] = _dot(a_ref[...], b_ref[...]).astype(o_ref.dtype)


def _matmul(a, b, out_dtype, *, tm, tn, name):
    m, k = a.shape
    n = b.shape[1]
    tm, tn = _tile(m, tm, SUBLANES), _tile(n, tn, LANES)
    return pl.pallas_call(
        _mm_kernel,
        out_shape=jax.ShapeDtypeStruct((m, n), out_dtype),
        grid=(m // tm, n // tn),
        in_specs=[pl.BlockSpec((tm, k), lambda i, j: (i, 0)),
                  pl.BlockSpec((k, tn), lambda i, j: (0, j))],
        out_specs=pl.BlockSpec((tm, tn), lambda i, j: (i, j)),
        compiler_params=_cparams("parallel", "parallel"),
        name=name,
    )(a, b)


def _mm_t_kernel(a_ref, bt_ref, o_ref):
    o_ref[...] = _dot_nt(bt_ref[...], a_ref[...]).astype(o_ref.dtype)


def _matmul_t(a, bt, out_dtype, *, tm, tn, name):
    m, k = a.shape
    n = bt.shape[0]
    tm, tn = _tile(m, tm, SUBLANES), _tile(n, tn, LANES)
    return pl.pallas_call(
        _mm_t_kernel,
        out_shape=jax.ShapeDtypeStruct((n, m), out_dtype),
        grid=(m // tm, n // tn),
        in_specs=[pl.BlockSpec((tm, k), lambda i, j: (i, 0)),
                  pl.BlockSpec((tn, k), lambda i, j: (j, 0))],
        out_specs=pl.BlockSpec((tn, tm), lambda i, j: (j, i)),
        compiler_params=_cparams("parallel", "parallel"),
        name=name,
    )(a, bt)


def _gate_kernel(z_ref, b_ref, o_ref, carry_ref, *, fh, mh, ts):
    @pl.when(pl.program_id(1) == 0)
    def _():
        carry_ref[...] = jnp.zeros_like(carry_ref)

    z = z_ref[...] + b_ref[...]
    ls = _log_sigmoid(z)
    lane = lax.broadcasted_iota(jnp.int32, z.shape, 1)
    row = lax.broadcasted_iota(jnp.int32, (ts, ts), 0)
    col = lax.broadcasted_iota(jnp.int32, (ts, ts), 1)
    tri = jnp.where(col <= row, 1.0, 0.0).astype(BF16)
    cum = _dot3_left(tri, ls) + carry_ref[...]
    carry_ref[...] = cum[ts - 1:ts, :]
    o_ref[...] = jnp.where(lane < fh, cum, jnp.where(lane < fh + mh, z, ls))


def _gate_prep(z, bias, *, batch, fh, mh):
    t = z.shape[0]
    s = t // batch
    ts = min(256, s)
    ns = s // ts
    return pl.pallas_call(
        functools.partial(_gate_kernel, fh=fh, mh=mh, ts=ts),
        out_shape=jax.ShapeDtypeStruct((t, LANES), F32),
        grid=(batch, ns),
        in_specs=[pl.BlockSpec((ts, LANES), lambda b, i: (b * ns + i, 0)),
                  pl.BlockSpec((1, LANES), lambda b, i: (0, 0))],
        out_specs=pl.BlockSpec((ts, LANES), lambda b, i: (b * ns + i, 0)),
        scratch_shapes=[pltpu.VMEM((1, LANES), F32)],
        compiler_params=_cparams("parallel", "arbitrary"),
        name="gate_prep",
    )(z, bias)


def _fox_kernel(qt_ref, k_ref, vt_ref, ck_ref, cq_ref, o_ref, m_ref, acc_ref, r_ref, kx_ref, *, tq):
    qi = pl.program_id(2)
    dh = HEAD_DIM

    @pl.when(qi == 0)
    def _():
        hi, mid, lo = _split3(ck_ref[...] * LOG2E)
        lane = lax.broadcasted_iota(jnp.int32, hi.shape, 1)
        zero = jnp.zeros_like(hi)
        kx_ref[:, :dh] = k_ref[...]
        kx_ref[:, dh:] = jnp.where(lane == 0, hi, jnp.where(lane == 1, mid, jnp.where(lane == 2, lo, zero)))

    sub = lax.broadcasted_iota(jnp.int32, (dh, tq), 0)
    qx = jnp.concatenate([qt_ref[...], jnp.where(sub < 3, -1.0, 0.0).astype(BF16)], axis=0)
    cq = cq_ref[...] * LOG2E
    ones = jnp.ones((SUBLANES, tq), BF16)
    m_ref[...] = jnp.full_like(m_ref, NEG_INF)
    acc_ref[...] = jnp.zeros_like(acc_ref)

    def scores(kj, slot):
        start = pl.multiple_of(kj * tq, tq)
        r_ref[slot] = _dot(kx_ref[pl.ds(start, tq), :], qx)

    def update(kj, slot, masked):
        start = pl.multiple_of(kj * tq, tq)
        vt1 = jnp.concatenate([vt_ref[:, pl.ds(start, tq)], ones], axis=0)
        r = r_ref[slot]
        if masked:
            key = lax.broadcasted_iota(jnp.int32, (tq, tq), 0)
            qry = lax.broadcasted_iota(jnp.int32, (tq, tq), 1)
            r = jnp.where(key <= qry, r, NEG_INF)
        m_prev = m_ref[...]
        m_new = jnp.maximum(m_prev, jnp.max(r, axis=0, keepdims=True) + cq)
        p = jnp.exp2(r + (cq - m_new))
        alpha = jnp.exp2(m_prev - m_new)
        acc_ref[...] = alpha * acc_ref[...] + _dot(vt1, p.astype(BF16))
        m_ref[...] = m_new

    scores(0, 0)

    def body(pair, c):
        scores(2 * pair + 1, 1)
        update(2 * pair, 0, False)
        scores(2 * pair + 2, 0)
        update(2 * pair + 1, 1, False)
        return c

    lax.fori_loop(0, qi // 2, body, 0)

    @pl.when(qi % 2 == 1)
    def _():
        scores(qi, 1)
        update(qi - 1, 0, False)
        update(qi, 1, True)

    @pl.when(qi % 2 == 0)
    def _():
        update(qi, 0, True)

    o_ref[...] = (acc_ref[:dh, :] / acc_ref[dh:dh + 1, :]).T.astype(o_ref.dtype)


def _fox_attention(qv_t, proj, cum_rep, cum_row, *, batch, fh, tq):
    t = proj.shape[0]
    s = t // batch
    tq = _tile(s, tq, LANES)
    nq = s // tq
    return pl.pallas_call(
        functools.partial(_fox_kernel, tq=tq),
        out_shape=jax.ShapeDtypeStruct((t, fh * HEAD_DIM), BF16),
        grid=(batch, fh, nq),
        in_specs=[pl.BlockSpec((HEAD_DIM, tq), lambda b, h, i: (h, b * nq + i)),
                  pl.BlockSpec((s, HEAD_DIM), lambda b, h, i: (b, h)),
                  pl.BlockSpec((HEAD_DIM, s), lambda b, h, i: (fh + h, b)),
                  pl.BlockSpec((None, s, LANES), lambda b, h, i: (h, b, 0)),
                  pl.BlockSpec((None, 1, tq), lambda b, h, i: (h, 0, b * nq + i))],
        out_specs=pl.BlockSpec((tq, HEAD_DIM), lambda b, h, i: (b * nq + i, h)),
        scratch_shapes=[pltpu.VMEM((1, tq), F32), pltpu.VMEM((HEAD_DIM + SUBLANES, tq), F32),
                        pltpu.VMEM((2, tq, tq), F32), pltpu.VMEM((s, 2 * HEAD_DIM), BF16)],
        compiler_params=_cparams("parallel", "parallel", "arbitrary"),
        name="fox_attention",
    )(qv_t, proj, qv_t, cum_rep, cum_row)


def _causal_conv(x, ext_ref, w_ref, b_ref, ts, first):
    @pl.when(first)
    def _():
        ext_ref[0:SUBLANES, :] = jnp.zeros((SUBLANES, x.shape[1]), F32)

    ext_ref[SUBLANES:SUBLANES + ts, :] = x
    y = b_ref[...] + w_ref[CONV_WIDTH - 1:CONV_WIDTH, :] * x
    for k in range(CONV_WIDTH - 1):
        off = SUBLANES - (CONV_WIDTH - 1) + k
        y = y + w_ref[k:k + 1, :] * ext_ref[off:off + ts, :]
    ext_ref[0:SUBLANES, :] = ext_ref[ts:ts + SUBLANES, :]
    return y


def _lru_kernel(lx_ref, lg_ref, cw_ref, cb_ref, wa_ref, ba_ref, wx_ref, bx_ref, lam_ref, o_ref,
                ext_ref, a_ref, u_ref, h_ref, *, ts, nblk):
    first = pl.program_id(1) == 0

    @pl.when(first)
    def _():
        h_ref[...] = jnp.zeros_like(h_ref)

    xl = _causal_conv(lx_ref[...].astype(F32), ext_ref, cw_ref, cb_ref, ts, first)
    lam = lam_ref[...]
    nsp = jnp.maximum(-lam, 0.0) + jnp.log1p(jnp.exp(-jnp.abs(lam)))
    for n in range(nblk):
        sl = slice(n * HEAD_DIM, (n + 1) * HEAD_DIM)
        xb = xl[:, sl]
        xb16 = xb.astype(BF16)
        r = _sigmoid(_dot(xb16, wa_ref[n]) + ba_ref[:, sl])
        gi = _sigmoid(_dot(xb16, wx_ref[n]) + bx_ref[:, sl])
        log_a = (-LRU_C) * r * nsp[:, sl]
        a_ref[:, sl] = jnp.exp(log_a)
        u_ref[:, sl] = jnp.sqrt(-_expm1(2.0 * log_a)) * gi * xb

    def group(g, h):
        base = pl.multiple_of(g * SUBLANES, SUBLANES)
        a8 = a_ref[pl.ds(base, SUBLANES), :]
        u8 = u_ref[pl.ds(base, SUBLANES), :]
        rows = []
        for r in range(SUBLANES):
            h = a8[r:r + 1, :] * h + u8[r:r + 1, :]
            rows.append(h)
        u_ref[pl.ds(base, SUBLANES), :] = jnp.concatenate(rows, axis=0)
        return h

    h_ref[...] = lax.fori_loop(0, ts // SUBLANES, group, h_ref[...])
    o_ref[...] = (u_ref[...] * _gelu_tanh(lg_ref[...].astype(F32))).astype(o_ref.dtype)


def _lru_branch(proj, conv_w, conv_b, w_a, b_a, w_x, b_x, lam, *, batch, bw, ts):
    t = proj.shape[0]
    s = t // batch
    ts = _tile(s, ts, CHUNK)
    ns = s // ts
    nblk = bw // HEAD_DIM
    row = lambda b, i: b * ns + i
    full2 = lambda b, i: (0, 0)
    full3 = lambda b, i: (0, 0, 0)
    return pl.pallas_call(
        functools.partial(_lru_kernel, ts=ts, nblk=nblk),
        out_shape=jax.ShapeDtypeStruct((t, bw), BF16),
        grid=(batch, ns),
        in_specs=[pl.BlockSpec((ts, bw), lambda b, i: (row(b, i), PROJ_LX)),
                  pl.BlockSpec((ts, bw), lambda b, i: (row(b, i), PROJ_LG)),
                  pl.BlockSpec((CONV_WIDTH, bw), full2),
                  pl.BlockSpec((1, bw), full2),
                  pl.BlockSpec((nblk, HEAD_DIM, HEAD_DIM), full3),
                  pl.BlockSpec((1, bw), full2),
                  pl.BlockSpec((nblk, HEAD_DIM, HEAD_DIM), full3),
                  pl.BlockSpec((1, bw), full2),
                  pl.BlockSpec((1, bw), full2)],
        out_specs=pl.BlockSpec((ts, bw), lambda b, i: (row(b, i), 0)),
        scratch_shapes=[pltpu.VMEM((ts + SUBLANES, bw), F32), pltpu.VMEM((ts, bw), F32),
                        pltpu.VMEM((ts, bw), F32), pltpu.VMEM((1, bw), F32)],
        compiler_params=_cparams("parallel", "arbitrary"),
        name="rglru_branch",
    )(proj, proj, conv_w, conv_b, w_a, b_a, w_x, b_x, lam)


def _mlstm_kernel(qk_ref, v_ref, og_ref, g_ref, gt_ref, cw_ref, cb_ref, o_ref,
                  ext_ref, qk_s, c_ref, n_ref, m_ref, *, ts, mh, gate_lane):
    first = pl.program_id(1) == 0

    @pl.when(first)
    def _():
        c_ref[...] = jnp.zeros_like(c_ref)
        n_ref[...] = jnp.zeros_like(n_ref)
        m_ref[...] = jnp.zeros_like(m_ref)

    conv = _causal_conv(qk_ref[...].astype(F32), ext_ref, cw_ref, cb_ref, ts, first)
    qk_s[...] = conv * _sigmoid(conv)
    mqk = mh * HEAD_DIM
    kscale = HEAD_DIM ** -0.5

    row = lax.broadcasted_iota(jnp.int32, (CHUNK, CHUNK), 0)
    col = lax.broadcasted_iota(jnp.int32, (CHUNK, CHUNK), 1)
    tril = col <= row
    tri_lo = jnp.where(tril, 1.0, 0.0).astype(BF16)
    tri_up = jnp.where(row <= col, 1.0, 0.0).astype(BF16)

    def chunk(c, carry):
        r0 = pl.multiple_of(c * CHUNK, CHUNK)
        gates = g_ref[pl.ds(r0, CHUNK), :]
        gates_t = gt_ref[c]
        cum_c = _dot3_left(tri_lo, gates)
        cum_r = _dot3_right(gates_t, tri_up)
        for h in range(mh):
            li, lf = gate_lane + h, gate_lane + mh + h
            g_col = cum_c[:, lf:lf + 1]
            g_row = cum_r[lf:lf + 1, :]
            ig_col = gates[:, li:li + 1]
            ig_row = gates_t[li:li + 1, :]
            q = qk_s[pl.ds(r0, CHUNK), h * HEAD_DIM:(h + 1) * HEAD_DIM]
            k = qk_s[pl.ds(r0, CHUNK), mqk + h * HEAD_DIM:mqk + (h + 1) * HEAD_DIM] * kscale
            v = v_ref[pl.ds(r0, CHUNK), h * MLSTM_V_DIM:(h + 1) * MLSTM_V_DIM]
            q16 = q.astype(BF16)
            m_prev = m_ref[h]
            c_prev = c_ref[h]
            n_prev = n_ref[h]

            log_d = jnp.where(tril, g_col - g_row + ig_row, NEG_INF)
            m_inter = g_col + m_prev
            m_row = jnp.maximum(m_inter, jnp.max(log_d, axis=1, keepdims=True))
            sc = _dot_nt(q16, k.astype(BF16)) * jnp.exp(log_d - m_row)
            inter = jnp.exp(m_inter - m_row)
            num = _dot(sc.astype(BF16), v) + inter * _dot(q16, c_prev.astype(BF16))
            den = jnp.sum(sc, axis=1, keepdims=True) + inter * jnp.sum(q * n_prev, axis=1, keepdims=True)
            hh = num / jnp.maximum(jnp.abs(den), jnp.exp(-m_row))
            og = og_ref[pl.ds(r0, CHUNK), h * MLSTM_V_DIM:(h + 1) * MLSTM_V_DIM].astype(F32)
            o_ref[pl.ds(r0, CHUNK), h * MLSTM_V_DIM:(h + 1) * MLSTM_V_DIM] = (
                _sigmoid(og) * hh).astype(o_ref.dtype)

            g_last = g_col[CHUNK - 1:CHUNK, :]
            log_w = g_last - g_col + ig_col
            m_new = jnp.maximum(g_last + m_prev, jnp.max(log_w, axis=0, keepdims=True))
            w = jnp.exp(log_w - m_new)
            decay = jnp.exp(g_last + m_prev - m_new)
            wk = w * k
            c_ref[h] = decay * c_prev + _dot_tn(wk.astype(BF16), v)
            n_ref[h] = decay * n_prev + jnp.sum(wk, axis=0, keepdims=True)
            m_ref[h] = m_new
        return carry

    lax.fori_loop(0, ts // CHUNK, chunk, 0)


def _mlstm_branch(proj, gates, gates_t, conv_w, conv_b, *, batch, bw, fh, ts):
    t = proj.shape[0]
    s = t // batch
    ts = _tile(s, ts, CHUNK)
    ns = s // ts
    mh = bw // MLSTM_V_DIM
    row = lambda b, i: b * ns + i
    full2 = lambda b, i: (0, 0)
    kern = functools.partial(_mlstm_kernel, ts=ts, mh=mh, gate_lane=fh)
    return pl.pallas_call(
        kern,
        out_shape=jax.ShapeDtypeStruct((t, bw), BF16),
        grid=(batch, ns),
        in_specs=[pl.BlockSpec((ts, bw), lambda b, i: (row(b, i), PROJ_MQK)),
                  pl.BlockSpec((ts, bw), lambda b, i: (row(b, i), PROJ_MV)),
                  pl.BlockSpec((ts, bw), lambda b, i: (row(b, i), PROJ_MO)),
                  pl.BlockSpec((ts, LANES), lambda b, i: (row(b, i), 0)),
                  pl.BlockSpec((ts // CHUNK, LANES, CHUNK), lambda b, i: (row(b, i), 0, 0)),
                  pl.BlockSpec((CONV_WIDTH, bw), full2),
                  pl.BlockSpec((1, bw), full2)],
        out_specs=pl.BlockSpec((ts, bw), lambda b, i: (row(b, i), 0)),
        scratch_shapes=[pltpu.VMEM((ts + SUBLANES, bw), F32), pltpu.VMEM((ts, bw), F32),
                        pltpu.VMEM((mh, HEAD_DIM, MLSTM_V_DIM), F32),
                        pltpu.VMEM((mh, 1, HEAD_DIM), F32), pltpu.VMEM((mh, 1, 1), F32)],
        compiler_params=_cparams("parallel", "arbitrary"),
        name="mlstm_branch",
    )(proj, proj, proj, gates, gates_t, conv_w, conv_b)


def _merge_kernel(x_ref, y0_ref, y1_ref, y2_ref, wg_ref, bg_ref, wb_ref, o_ref, acc_ref):
    b = pl.program_id(2)
    gate = _sigmoid(_dot(x_ref[...], wg_ref[...]) + bg_ref[...])

    def contrib(y_ref):
        return gate * _dot(y_ref[...], wb_ref[...])

    @pl.when(b == 0)
    def _():
        acc_ref[...] = contrib(y0_ref)

    @pl.when(b == 1)
    def _():
        acc_ref[...] += contrib(y1_ref)

    @pl.when(b == 2)
    def _():
        o_ref[...] = (acc_ref[...] + contrib(y2_ref)).astype(o_ref.dtype)


def _merge(x16, ys, wg, bg, wb, *, tm, tn):
    t, d = x16.shape
    bw = ys[0].shape[1]
    tm, tn = _tile(t, tm, SUBLANES), _tile(d, tn, LANES)
    rows = lambda i, j, b: (i, 0)
    return pl.pallas_call(
        _merge_kernel,
        out_shape=jax.ShapeDtypeStruct((t, d), BF16),
        grid=(t // tm, d // tn, 3),
        in_specs=[_resident_spec((tm, d), rows),
                  _resident_spec((tm, bw), rows), _resident_spec((tm, bw), rows),
                  _resident_spec((tm, bw), rows),
                  pl.BlockSpec((None, d, tn), lambda i, j, b: (b, 0, j)),
                  pl.BlockSpec((None, 1, tn), lambda i, j, b: (b, 0, j)),
                  pl.BlockSpec((None, bw, tn), lambda i, j, b: (b, 0, j))],
        out_specs=pl.BlockSpec((tm, tn), lambda i, j, b: (i, j)),
        scratch_shapes=[pltpu.VMEM((tm, tn), F32)],
        compiler_params=_cparams("parallel", "parallel", "arbitrary"),
        name="merge_branches",
    )(x16, ys[0], ys[1], ys[2], wg, bg, wb)


def _ln_kernel(r_ref, d_ref, g_ref, b_ref, o_ref, o16_ref, *, alpha):
    x = alpha * r_ref[...] + d_ref[...]
    mu = jnp.mean(x, axis=-1, keepdims=True)
    xc = x - mu
    var = jnp.mean(xc * xc, axis=-1, keepdims=True)
    y = xc * lax.rsqrt(var + LN_EPS) * g_ref[...] + b_ref[...]
    o_ref[...] = y
    o16_ref[...] = y.astype(BF16)


def _post_norm(resid, delta, g, b, *, alpha, tr):
    t, d = resid.shape
    tr = _tile(t, tr, SUBLANES)
    rows = lambda i: (i, 0)
    return pl.pallas_call(
        functools.partial(_ln_kernel, alpha=alpha),
        out_shape=(jax.ShapeDtypeStruct((t, d), F32), jax.ShapeDtypeStruct((t, d), BF16)),
        grid=(t // tr,),
        in_specs=[pl.BlockSpec((tr, d), rows), pl.BlockSpec((tr, d), rows),
                  pl.BlockSpec((1, d), lambda i: (0, 0)), pl.BlockSpec((1, d), lambda i: (0, 0))],
        out_specs=(pl.BlockSpec((tr, d), rows), pl.BlockSpec((tr, d), rows)),
        compiler_params=_cparams("parallel"),
        name="post_norm",
    )(resid, delta, g, b)


def _topk_rows(arrays, k):
    n = arrays[0].shape[1]
    kid = lax.broadcasted_iota(jnp.int32, (k, n), 0)
    rids = [lax.broadcasted_iota(jnp.int32, a.shape, 0) for a in arrays]

    def step(i, carry):
        out = []
        for (a, vals, idx), rid in zip(carry, rids):
            m = jnp.max(a, axis=0, keepdims=True)
            first = jnp.min(jnp.where(a == m, rid, a.shape[0]), axis=0, keepdims=True)
            vals = jnp.where(kid == i, m, vals)
            idx = jnp.where(kid == i, first, idx)
            a = jnp.where(rid == first, NEG_INF, a)
            out.append((a, vals, idx))
        return tuple(out)

    init = tuple((a, jnp.zeros((k, n), F32), jnp.zeros((k, n), jnp.int32)) for a in arrays)
    res = lax.fori_loop(0, k, step, init)
    return [(vals, idx) for _, vals, idx in res]


def _pair_candidates(v1, v2):
    k, n = v1.shape
    assert k % SUBLANES == 0
    groups, table = [], []
    neg = jnp.full((SUBLANES, n), NEG_INF, F32)
    sub = lax.broadcasted_iota(jnp.int32, (SUBLANES, n), 0)
    a = 0
    while a < k:
        nb = k // (a + 1)
        if nb > 1:
            for b0 in range(0, nb, SUBLANES):
                g = v1[a:a + 1, :] + v2[b0:b0 + SUBLANES, :]
                groups.append(g if nb - b0 >= SUBLANES else jnp.where(sub < nb - b0, g, neg))
                table.append((a, 0, b0, 1))
            a += 1
        else:
            assert a % SUBLANES == 0
            groups.append(v1[a:a + SUBLANES, :] + v2[0:1, :])
            table.append((a, 1, 0, 0))
            a += SUBLANES
    return jnp.concatenate(groups, axis=0), table


def _pair_of_row(table, pos):
    g = jnp.right_shift(pos, SUBLANES.bit_length() - 1)
    r = jnp.bitwise_and(pos, SUBLANES - 1)
    a = jnp.zeros(pos.shape, jnp.int32)
    b = jnp.zeros(pos.shape, jnp.int32)
    for gi, (a0, da, b0, db) in enumerate(table):
        hit = g == gi
        a = jnp.where(hit, a0 + da * r, a)
        b = jnp.where(hit, b0 + db * r, b)
    return a, b


def _take_rows(table, sel):
    k, n = table.shape
    out = jnp.zeros((k, n), jnp.int32)
    for a in range(k):
        out = jnp.where(sel == a, table[a:a + 1, :], out)
    return out


def _route_kernel(q_ref, keys_ref, i_ref, j_ref, g_ref):
    q = q_ref[...]
    s1 = _dot_nt(keys_ref[0], q[:, :PEER_HALF])
    s2 = _dot_nt(keys_ref[1], q[:, PEER_HALF:])
    (v1, i1), (v2, i2) = _topk_rows([s1, s2], PEER_TOPK)
    cand, table = _pair_candidates(v1, v2)
    ((top, pos),) = _topk_rows([cand], PEER_TOPK)
    e = jnp.exp(top - top[0:1, :])
    g_ref[...] = e / jnp.sum(e, axis=0, keepdims=True)
    a, b = _pair_of_row(table, pos)
    i_ref[...] = _take_rows(i1, a)
    j_ref[...] = _take_rows(i2, b)


def _peer_route(q16, keys16, *, tt):
    t = q16.shape[0]
    ph = q16.shape[1] // (2 * PEER_HALF)
    tt = _tile(t, tt, LANES)
    n = ph * PEER_TOPK
    out = jax.ShapeDtypeStruct((n, t), jnp.int32)
    ospec = pl.BlockSpec((PEER_TOPK, tt), lambda i, h: (h, i))
    return pl.pallas_call(
        _route_kernel,
        out_shape=(out, out, jax.ShapeDtypeStruct((n, t), F32)),
        grid=(t // tt, ph),
        in_specs=[pl.BlockSpec((tt, 2 * PEER_HALF), lambda i, h: (i, h)),
                  pl.BlockSpec((2, PEER_KEYS, PEER_HALF), lambda i, h: (0, 0, 0))],
        out_specs=(ospec, ospec, ospec),
        compiler_params=_cparams("parallel", "parallel"),
        name="peer_route",
    )(q16, keys16)


def _gates_kernel(i_ref, j_ref, g_ref, o_ref, scr_ref, *, tb, pitch):
    n = i_ref.shape[1]
    sub = lax.broadcasted_iota(jnp.int32, (PEER_KEYS, n), 0)

    def token(t, c):
        irow = i_ref[pl.ds(t, 1), :]
        jrow = j_ref[pl.ds(t, 1), :]
        grow = g_ref[pl.ds(t, 1), :]
        a = jnp.where(sub == irow, grow, 0.0).astype(BF16)
        b = jnp.where(sub == jrow, 1.0, 0.0).astype(BF16)
        scr_ref[pl.ds(t, PEER_KEYS, stride=pitch), :] = _dot_nt(a, b)
        return c

    lax.fori_loop(0, tb, token, 0, unroll=GATE_TOKEN_UNROLL)
    for i in range(PEER_KEYS):
        o_ref[:, i * PEER_KEYS:(i + 1) * PEER_KEYS] = (
            scr_ref[i * pitch:i * pitch + tb, :].astype(o_ref.dtype))


def _peer_gates(sel_i, sel_j, gates, *, tb):
    t, n = sel_i.shape
    tb = _tile(t, tb, SUBLANES)
    e = PEER_KEYS * PEER_KEYS
    rows = lambda i: (i, 0)
    pitch = tb + SUBLANES if (tb // SUBLANES) % 2 == 0 else tb
    return pl.pallas_call(
        functools.partial(_gates_kernel, tb=tb, pitch=pitch),
        out_shape=jax.ShapeDtypeStruct((t, e), BF16),
        grid=(t // tb,),
        in_specs=[pl.BlockSpec((tb, n), rows), pl.BlockSpec((tb, n), rows), pl.BlockSpec((tb, n), rows)],
        out_specs=pl.BlockSpec((tb, e), rows),
        scratch_shapes=[pltpu.VMEM((pitch * PEER_KEYS, PEER_KEYS), F32)],
        compiler_params=_cparams("parallel"),
        name="peer_gates",
    )(sel_i, sel_j, gates)


def _expert_kernel(x_ref, ut_ref, v_ref, w_ref, o_ref, *, tc):
    @pl.when(pl.program_id(1) == 0)
    def _():
        o_ref[...] = jnp.zeros_like(o_ref)

    act = _gelu_tanh(_dot(x_ref[...], ut_ref[...]))
    p = (w_ref[...].astype(F32) * act).astype(BF16)
    for c in range(0, o_ref.shape[1], tc):
        o_ref[:, c:c + tc] += _dot(p, v_ref[:, c:c + tc])


def _peer_experts(x16, ut16, v16, w16, *, tt, te):
    t, d = x16.shape
    e = ut16.shape[1]
    tt, te = _tile(t, tt, SUBLANES), _tile(e, te, LANES)
    return pl.pallas_call(
        functools.partial(_expert_kernel, tc=_tile(d, 1024, LANES)),
        out_shape=jax.ShapeDtypeStruct((t, d), F32),
        grid=(t // tt, e // te),
        in_specs=[_resident_spec((tt, d), lambda i, j: (i, 0)),
                  pl.BlockSpec((d, te), lambda i, j: (0, j)),
                  pl.BlockSpec((te, d), lambda i, j: (j, 0)),
                  pl.BlockSpec((tt, te), lambda i, j: (i, j))],
        out_specs=_resident_spec((tt, d), lambda i, j: (i, 0)),
        compiler_params=_cparams("parallel", "arbitrary"),
        name="peer_experts",
    )(x16, ut16, v16, w16)


def _mixer(h16, batch, w_in, fox_f_bias, lru_conv_w, lru_conv_b, lru_gate_a_w, lru_gate_a_b,
           lru_gate_x_w, lru_gate_x_b, lru_lambda, mlstm_conv_w, mlstm_conv_b, mlstm_i_bias,
           mlstm_f_bias, w_branch, w_merge_gate, b_merge_gate, w_out):
    t, d = h16.shape
    bw = w_branch.shape[1]
    fh = bw // HEAD_DIM
    mh = bw // MLSTM_V_DIM
    s = t // batch

    o_ff = 3 * bw
    o_lx = o_ff + fh
    o_mi = o_lx + 5 * bw
    q_scale = HEAD_DIM ** -0.5 * LOG2E
    w_qv_t = jnp.concatenate([w_in[:, :bw] * q_scale, w_in[:, 2 * bw:o_ff]], axis=1).T.astype(BF16)
    w_main = jnp.concatenate([w_in[:, bw:2 * bw], w_in[:, o_lx:o_mi]], axis=1).astype(BF16)
    n_gate = fh + 2 * mh
    w_gate = jnp.concatenate([w_in[:, o_ff:o_lx], w_in[:, o_mi:]], axis=1)
    w_gate = jnp.pad(w_gate, ((0, 0), (0, LANES - n_gate))).astype(BF16)
    gate_bias = jnp.pad(jnp.concatenate([fox_f_bias, mlstm_i_bias, mlstm_f_bias]),
                        (0, LANES - n_gate)).reshape(1, LANES)

    proj = _matmul(h16, w_main, BF16, tm=1024, tn=1024, name="in_proj")
    gate_z = _matmul(h16, w_gate, F32, tm=1024, tn=LANES, name="gate_proj")
    gates = _gate_prep(gate_z, gate_bias, batch=batch, fh=fh, mh=mh)
    qv_t = _matmul_t(h16, w_qv_t, BF16, tm=1024, tn=1024, name="qv_proj_t")
    cum = gates.T[:fh]
    cum_rep = jnp.broadcast_to(cum[:, :, None], (fh, t, LANES))
    y_fox = _fox_attention(qv_t, proj, cum_rep, cum.reshape(fh, 1, t), batch=batch, fh=fh, tq=512)
    y_lru = _lru_branch(proj, lru_conv_w, lru_conv_b.reshape(1, bw), lru_gate_a_w.astype(BF16),
                        lru_gate_a_b.reshape(1, bw), lru_gate_x_w.astype(BF16),
                        lru_gate_x_b.reshape(1, bw), lru_lambda.reshape(1, bw),
                        batch=batch, bw=bw, ts=256)
    gates_ct = gates.reshape(t // CHUNK, CHUNK, LANES).transpose(0, 2, 1)
    y_mlstm = _mlstm_branch(proj, gates, gates_ct, mlstm_conv_w, mlstm_conv_b.reshape(1, bw),
                            batch=batch, bw=bw, fh=fh, ts=512)
    merged = _merge(h16, (y_fox, y_lru, y_mlstm), w_merge_gate.astype(BF16),
                    b_merge_gate.reshape(3, 1, d), w_branch.astype(BF16), tm=1024, tn=512)
    return _matmul(merged, w_out.astype(BF16), F32, tm=1024, tn=1024, name="out_proj")


def _peer(h16, w_query, sub_keys, expert_u, expert_v):
    q16 = _matmul(h16, w_query.astype(BF16), BF16, tm=1024, tn=1024, name="peer_query")
    sel_i, sel_j, gates = _peer_route(q16, sub_keys.astype(BF16), tt=512)
    w16 = _peer_gates(sel_i.T, sel_j.T, gates.T, tb=128)
    return _peer_experts(h16, expert_u.T.astype(BF16), expert_v.astype(BF16), w16, tt=1024, te=256)


def kernel(x, w_in, fox_f_bias, lru_conv_w, lru_conv_b, lru_gate_a_w, lru_gate_a_b, lru_gate_x_w, lru_gate_x_b, lru_lambda, mlstm_conv_w, mlstm_conv_b, mlstm_i_bias, mlstm_f_bias, w_branch, w_merge_gate, b_merge_gate, w_out, ln1_g, ln1_b, peer_w_query, peer_sub_keys, peer_u, peer_v, ln2_g, ln2_b):
    batch, seq, d = x.shape
    depth = w_in.shape[0]
    alpha = (2.0 * depth) ** 0.25
    h = x.reshape(batch * seq, d)
    h16 = h.astype(BF16)
    for l in range(depth):
        mix = _mixer(h16, batch, w_in[l], fox_f_bias[l], lru_conv_w[l], lru_conv_b[l], lru_gate_a_w[l],
                     lru_gate_a_b[l], lru_gate_x_w[l], lru_gate_x_b[l], lru_lambda[l], mlstm_conv_w[l],
                     mlstm_conv_b[l], mlstm_i_bias[l], mlstm_f_bias[l], w_branch[l], w_merge_gate[l],
                     b_merge_gate[l], w_out[l])
        h, h16 = _post_norm(h, mix, ln1_g[l].reshape(1, d), ln1_b[l].reshape(1, d), alpha=alpha, tr=256)
        ffn = _peer(h16, peer_w_query[l], peer_sub_keys[l], peer_u[l], peer_v[l])
        h, h16 = _post_norm(h, ffn, ln2_g[l].reshape(1, d), ln2_b[l].reshape(1, d), alpha=alpha, tr=256)
    return h.reshape(batch, seq, d)
```

```python
import functools
import math

import jax
import jax.numpy as jnp
from jax import lax
from jax.experimental import pallas as pl
from jax.experimental.pallas import tpu as pltpu

F32 = jnp.float32
BF16 = jnp.bfloat16

HEAD_DIM = 128
MLSTM_V_DIM = 256
CHUNK = 64
CONV_WIDTH = 4
LRU_C = 8.0
PEER_KEYS = 128
PEER_HALF = 128
PEER_TOPK = 16
LN_EPS = 1e-5
GATE_TOKEN_UNROLL = 16

V7X_VMEM_BYTES = 64 * 1024 * 1024
VMEM_LIMIT = V7X_VMEM_BYTES - 8 * 1024 * 1024
LANES = 128
SUBLANES = 8

NEG_INF = float("-inf")
LOG2E = math.log2(math.e)

PROJ_FK, PROJ_LX, PROJ_LG, PROJ_MQK, PROJ_MV, PROJ_MO = range(6)


def _tile(dim, pref, align):
    if dim <= pref:
        return dim
    t = pref - pref % align
    while dim % t:
        t -= align
    assert t > 0, (dim, pref, align)
    return t


def _resident_spec(block_shape, index_map):
    return pl.BlockSpec(block_shape, index_map, pipeline_mode=pl.Buffered(1))


def _cparams(*sem):
    return pltpu.CompilerParams(dimension_semantics=sem, vmem_limit_bytes=VMEM_LIMIT)


def _log_sigmoid(x):
    return jnp.minimum(x, 0.0) - jnp.log1p(jnp.exp(-jnp.abs(x)))


def _sigmoid(x):
    return 1.0 / (1.0 + jnp.exp(-x))


def _gelu_tanh(x):
    c = math.sqrt(2.0 / math.pi)
    return 0.5 * x * (1.0 + jnp.tanh(c * (x + 0.044715 * (x * x * x))))


def _expm1(z):
    u = jnp.exp(z)
    um1 = u - 1.0
    near = um1 * z / jnp.log(u)
    return jnp.where(jnp.abs(z) > 0.5, um1, jnp.where(u == 1.0, z, near))


def _split3(x):
    hi = x.astype(BF16)
    r1 = x - hi.astype(F32)
    mid = r1.astype(BF16)
    lo = (r1 - mid.astype(F32)).astype(BF16)
    return hi, mid, lo


def _dot(a, b):
    return jnp.dot(a, b, preferred_element_type=F32)


def _dot_nt(a, b):
    return lax.dot_general(a, b, (((1,), (1,)), ((), ())), preferred_element_type=F32)


def _dot_tn(a, b):
    return lax.dot_general(a, b, (((0,), (0,)), ((), ())), preferred_element_type=F32)


def _dot3_left(tri, x):
    hi, mid, lo = _split3(x)
    return _dot(tri, hi) + _dot(tri, mid) + _dot(tri, lo)


def _dot3_right(x, tri):
    hi, mid, lo = _split3(x)
    return _dot(hi, tri) + _dot(mid, tri) + _dot(lo, tri)


def _mm_kernel(a_ref, b_ref, o_ref):
    o_ref[...] = _dot(a_ref[...], b_ref[...]).astype(o_ref.dtype)


def _matmul(a, b, out_dtype, *, tm, tn, name):
    m, k = a.shape
    n = b.shape[1]
    tm, tn = _tile(m, tm, SUBLANES), _tile(n, tn, LANES)
    return pl.pallas_call(
        _mm_kernel,
        out_shape=jax.ShapeDtypeStruct((m, n), out_dtype),
        grid=(m // tm, n // tn),
        in_specs=[pl.BlockSpec((tm, k), lambda i, j: (i, 0)),
                  pl.BlockSpec((k, tn), lambda i, j: (0, j))],
        out_specs=pl.BlockSpec((tm, tn), lambda i, j: (i, j)),
        compiler_params=_cparams("parallel", "parallel"),
        name=name,
    )(a, b)


def _mm_t_kernel(a_ref, bt_ref, o_ref):
    o_ref[...] = _dot_nt(bt_ref[...], a_ref[...]).astype(o_ref.dtype)


def _matmul_t(a, bt, out_dtype, *, tm, tn, name):
    m, k = a.shape
    n = bt.shape[0]
    tm, tn = _tile(m, tm, SUBLANES), _tile(n, tn, LANES)
    return pl.pallas_call(
        _mm_t_kernel,
        out_shape=jax.ShapeDtypeStruct((n, m), out_dtype),
        grid=(m // tm, n // tn),
        in_specs=[pl.BlockSpec((tm, k), lambda i, j: (i, 0)),
                  pl.BlockSpec((tn, k), lambda i, j: (j, 0))],
        out_specs=pl.BlockSpec((tn, tm), lambda i, j: (j, i)),
        compiler_params=_cparams("parallel", "parallel"),
        name=name,
    )(a, bt)


def _gate_kernel(z_ref, b_ref, o_ref, carry_ref, *, fh, mh, ts):
    @pl.when(pl.program_id(1) == 0)
    def _():
        carry_ref[...] = jnp.zeros_like(carry_ref)

    z = z_ref[...] + b_ref[...]
    ls = _log_sigmoid(z)
    lane = lax.broadcasted_iota(jnp.int32, z.shape, 1)
    row = lax.broadcasted_iota(jnp.int32, (ts, ts), 0)
    col = lax.broadcasted_iota(jnp.int32, (ts, ts), 1)
    tri = jnp.where(col <= row, 1.0, 0.0).astype(BF16)
    cum = _dot3_left(tri, ls) + carry_ref[...]
    carry_ref[...] = cum[ts - 1:ts, :]
    o_ref[...] = jnp.where(lane < fh, cum, jnp.where(lane < fh + mh, z, ls))


def _gate_prep(z, bias, *, batch, fh, mh):
    t = z.shape[0]
    s = t // batch
    ts = min(256, s)
    ns = s // ts
    return pl.pallas_call(
        functools.partial(_gate_kernel, fh=fh, mh=mh, ts=ts),
        out_shape=jax.ShapeDtypeStruct((t, LANES), F32),
        grid=(batch, ns),
        in_specs=[pl.BlockSpec((ts, LANES), lambda b, i: (b * ns + i, 0)),
                  pl.BlockSpec((1, LANES), lambda b, i: (0, 0))],
        out_specs=pl.BlockSpec((ts, LANES), lambda b, i: (b * ns + i, 0)),
        scratch_shapes=[pltpu.VMEM((1, LANES), F32)],
        compiler_params=_cparams("parallel", "arbitrary"),
        name="gate_prep",
    )(z, bias)


def _fox_kernel(qt_ref, k_ref, vt_ref, ck_ref, cq_ref, o_ref, m_ref, acc_ref, r_ref, kx_ref, *, tq):
    qi = pl.program_id(2)
    dh = HEAD_DIM

    @pl.when(qi == 0)
    def _():
        hi, mid, lo = _split3(ck_ref[...] * LOG2E)
        lane = lax.broadcasted_iota(jnp.int32, hi.shape, 1)
        zero = jnp.zeros_like(hi)
        kx_ref[:, :dh] = k_ref[...]
        kx_ref[:, dh:] = jnp.where(lane == 0, hi, jnp.where(lane == 1, mid, jnp.where(lane == 2, lo, zero)))

    sub = lax.broadcasted_iota(jnp.int32, (dh, tq), 0)
    qx = jnp.concatenate([qt_ref[...], jnp.where(sub < 3, -1.0, 0.0).astype(BF16)], axis=0)
    cq = cq_ref[...] * LOG2E
    ones = jnp.ones((SUBLANES, tq), BF16)
    m_ref[...] = jnp.full_like(m_ref, NEG_INF)
    acc_ref[...] = jnp.zeros_like(acc_ref)

    def scores(kj, slot):
        start = pl.multiple_of(kj * tq, tq)
        r_ref[slot] = _dot(kx_ref[pl.ds(start, tq), :], qx)

    def update(kj, slot, masked):
        start = pl.multiple_of(kj * tq, tq)
        vt1 = jnp.concatenate([vt_ref[:, pl.ds(start, tq)], ones], axis=0)
        r = r_ref[slot]
        if masked:
            key = lax.broadcasted_iota(jnp.int32, (tq, tq), 0)
            qry = lax.broadcasted_iota(jnp.int32, (tq, tq), 1)
            r = jnp.where(key <= qry, r, NEG_INF)
        m_prev = m_ref[...]
        m_new = jnp.maximum(m_prev, jnp.max(r, axis=0, keepdims=True) + cq)
        p = jnp.exp2(r + (cq - m_new))
        alpha = jnp.exp2(m_prev - m_new)
        acc_ref[...] = alpha * acc_ref[...] + _dot(vt1, p.astype(BF16))
        m_ref[...] = m_new

    scores(0, 0)

    def body(pair, c):
        scores(2 * pair + 1, 1)
        update(2 * pair, 0, False)
        scores(2 * pair + 2, 0)
        update(2 * pair + 1, 1, False)
        return c

    lax.fori_loop(0, qi // 2, body, 0)

    @pl.when(qi % 2 == 1)
    def _():
        scores(qi, 1)
        update(qi - 1, 0, False)
        update(qi, 1, True)

    @pl.when(qi % 2 == 0)
    def _():
        update(qi, 0, True)

    o_ref[...] = (acc_ref[:dh, :] / acc_ref[dh:dh + 1, :]).T.astype(o_ref.dtype)


def _fox_attention(qv_t, proj, cum_rep, cum_row, *, batch, fh, tq):
    t = proj.shape[0]
    s = t // batch
    tq = _tile(s, tq, LANES)
    nq = s // tq
    return pl.pallas_call(
        functools.partial(_fox_kernel, tq=tq),
        out_shape=jax.ShapeDtypeStruct((t, fh * HEAD_DIM), BF16),
        grid=(batch, fh, nq),
        in_specs=[pl.BlockSpec((HEAD_DIM, tq), lambda b, h, i: (h, b * nq + i)),
                  pl.BlockSpec((s, HEAD_DIM), lambda b, h, i: (b, h)),
                  pl.BlockSpec((HEAD_DIM, s), lambda b, h, i: (fh + h, b)),
                  pl.BlockSpec((None, s, LANES), lambda b, h, i: (h, b, 0)),
                  pl.BlockSpec((None, 1, tq), lambda b, h, i: (h, 0, b * nq + i))],
        out_specs=pl.BlockSpec((tq, HEAD_DIM), lambda b, h, i: (b * nq + i, h)),
        scratch_shapes=[pltpu.VMEM((1, tq), F32), pltpu.VMEM((HEAD_DIM + SUBLANES, tq), F32),
                        pltpu.VMEM((2, tq, tq), F32), pltpu.VMEM((s, 2 * HEAD_DIM), BF16)],
        compiler_params=_cparams("parallel", "parallel", "arbitrary"),
        name="fox_attention",
    )(qv_t, proj, qv_t, cum_rep, cum_row)


def _causal_conv(x, ext_ref, w_ref, b_ref, ts, first):
    @pl.when(first)
    def _():
        ext_ref[0:SUBLANES, :] = jnp.zeros((SUBLANES, x.shape[1]), F32)

    ext_ref[SUBLANES:SUBLANES + ts, :] = x
    y = b_ref[...] + w_ref[CONV_WIDTH - 1:CONV_WIDTH, :] * x
    for k in range(CONV_WIDTH - 1):
        off = SUBLANES - (CONV_WIDTH - 1) + k
        y = y + w_ref[k:k + 1, :] * ext_ref[off:off + ts, :]
    ext_ref[0:SUBLANES, :] = ext_ref[ts:ts + SUBLANES, :]
    return y


def _lru_kernel(lx_ref, lg_ref, cw_ref, cb_ref, wa_ref, ba_ref, wx_ref, bx_ref, lam_ref, o_ref,
                ext_ref, a_ref, u_ref, h_ref, *, ts, nblk):
    first = pl.program_id(1) == 0

    @pl.when(first)
    def _():
        h_ref[...] = jnp.zeros_like(h_ref)

    xl = _causal_conv(lx_ref[...].astype(F32), ext_ref, cw_ref, cb_ref, ts, first)
    lam = lam_ref[...]
    nsp = jnp.maximum(-lam, 0.0) + jnp.log1p(jnp.exp(-jnp.abs(lam)))
    for n in range(nblk):
        sl = slice(n * HEAD_DIM, (n + 1) * HEAD_DIM)
        xb = xl[:, sl]
        xb16 = xb.astype(BF16)
        r = _sigmoid(_dot(xb16, wa_ref[n]) + ba_ref[:, sl])
        gi = _sigmoid(_dot(xb16, wx_ref[n]) + bx_ref[:, sl])
        log_a = (-LRU_C) * r * nsp[:, sl]
        a_ref[:, sl] = jnp.exp(log_a)
        u_ref[:, sl] = jnp.sqrt(-_expm1(2.0 * log_a)) * gi * xb

    def group(g, h):
        base = pl.multiple_of(g * SUBLANES, SUBLANES)
        a8 = a_ref[pl.ds(base, SUBLANES), :]
        u8 = u_ref[pl.ds(base, SUBLANES), :]
        rows = []
        for r in range(SUBLANES):
            h = a8[r:r + 1, :] * h + u8[r:r + 1, :]
            rows.append(h)
        u_ref[pl.ds(base, SUBLANES), :] = jnp.concatenate(rows, axis=0)
        return h

    h_ref[...] = lax.fori_loop(0, ts // SUBLANES, group, h_ref[...])
    o_ref[...] = (u_ref[...] * _gelu_tanh(lg_ref[...].astype(F32))).astype(o_ref.dtype)


def _lru_branch(proj, conv_w, conv_b, w_a, b_a, w_x, b_x, lam, *, batch, bw, ts):
    t = proj.shape[0]
    s = t // batch
    ts = _tile(s, ts, CHUNK)
    ns = s // ts
    nblk = bw // HEAD_DIM
    row = lambda b, i: b * ns + i
    full2 = lambda b, i: (0, 0)
    full3 = lambda b, i: (0, 0, 0)
    return pl.pallas_call(
        functools.partial(_lru_kernel, ts=ts, nblk=nblk),
        out_shape=jax.ShapeDtypeStruct((t, bw), BF16),
        grid=(batch, ns),
        in_specs=[pl.BlockSpec((ts, bw), lambda b, i: (row(b, i), PROJ_LX)),
                  pl.BlockSpec((ts, bw), lambda b, i: (row(b, i), PROJ_LG)),
                  pl.BlockSpec((CONV_WIDTH, bw), full2),
                  pl.BlockSpec((1, bw), full2),
                  pl.BlockSpec((nblk, HEAD_DIM, HEAD_DIM), full3),
                  pl.BlockSpec((1, bw), full2),
                  pl.BlockSpec((nblk, HEAD_DIM, HEAD_DIM), full3),
                  pl.BlockSpec((1, bw), full2),
                  pl.BlockSpec((1, bw), full2)],
        out_specs=pl.BlockSpec((ts, bw), lambda b, i: (row(b, i), 0)),
        scratch_shapes=[pltpu.VMEM((ts + SUBLANES, bw), F32), pltpu.VMEM((ts, bw), F32),
                        pltpu.VMEM((ts, bw), F32), pltpu.VMEM((1, bw), F32)],
        compiler_params=_cparams("parallel", "arbitrary"),
        name="rglru_branch",
    )(proj, proj, conv_w, conv_b, w_a, b_a, w_x, b_x, lam)


def _mlstm_kernel(qk_ref, v_ref, og_ref, g_ref, gt_ref, cw_ref, cb_ref, o_ref,
                  ext_ref, qk_s, c_ref, n_ref, m_ref, *, ts, mh, gate_lane):
    first = pl.program_id(1) == 0

    @pl.when(first)
    def _():
        c_ref[...] = jnp.zeros_like(c_ref)
        n_ref[...] = jnp.zeros_like(n_ref)
        m_ref[...] = jnp.zeros_like(m_ref)

    conv = _causal_conv(qk_ref[...].astype(F32), ext_ref, cw_ref, cb_ref, ts, first)
    qk_s[...] = conv * _sigmoid(conv)
    mqk = mh * HEAD_DIM
    kscale = HEAD_DIM ** -0.5

    row = lax.broadcasted_iota(jnp.int32, (CHUNK, CHUNK), 0)
    col = lax.broadcasted_iota(jnp.int32, (CHUNK, CHUNK), 1)
    tril = col <= row
    tri_lo = jnp.where(tril, 1.0, 0.0).astype(BF16)
    tri_up = jnp.where(row <= col, 1.0, 0.0).astype(BF16)

    def chunk(c, carry):
        r0 = pl.multiple_of(c * CHUNK, CHUNK)
        gates = g_ref[pl.ds(r0, CHUNK), :]
        gates_t = gt_ref[c]
        cum_c = _dot3_left(tri_lo, gates)
        cum_r = _dot3_right(gates_t, tri_up)
        for h in range(mh):
            li, lf = gate_lane + h, gate_lane + mh + h
            g_col = cum_c[:, lf:lf + 1]
            g_row = cum_r[lf:lf + 1, :]
            ig_col = gates[:, li:li + 1]
            ig_row = gates_t[li:li + 1, :]
            q = qk_s[pl.ds(r0, CHUNK), h * HEAD_DIM:(h + 1) * HEAD_DIM]
            k = qk_s[pl.ds(r0, CHUNK), mqk + h * HEAD_DIM:mqk + (h + 1) * HEAD_DIM] * kscale
            v = v_ref[pl.ds(r0, CHUNK), h * MLSTM_V_DIM:(h + 1) * MLSTM_V_DIM]
            q16 = q.astype(BF16)
            m_prev = m_ref[h]
            c_prev = c_ref[h]
            n_prev = n_ref[h]

            log_d = jnp.where(tril, g_col - g_row + ig_row, NEG_INF)
            m_inter = g_col + m_prev
            m_row = jnp.maximum(m_inter, jnp.max(log_d, axis=1, keepdims=True))
            sc = _dot_nt(q16, k.astype(BF16)) * jnp.exp(log_d - m_row)
            inter = jnp.exp(m_inter - m_row)
            num = _dot(sc.astype(BF16), v) + inter * _dot(q16, c_prev.astype(BF16))
            den = jnp.sum(sc, axis=1, keepdims=True) + inter * jnp.sum(q * n_prev, axis=1, keepdims=True)
            hh = num / jnp.maximum(jnp.abs(den), jnp.exp(-m_row))
            og = og_ref[pl.ds(r0, CHUNK), h * MLSTM_V_DIM:(h + 1) * MLSTM_V_DIM].astype(F32)
            o_ref[pl.ds(r0, CHUNK), h * MLSTM_V_DIM:(h + 1) * MLSTM_V_DIM] = (
                _sigmoid(og) * hh).astype(o_ref.dtype)

            g_last = g_col[CHUNK - 1:CHUNK, :]
            log_w = g_last - g_col + ig_col
            m_new = jnp.maximum(g_last + m_prev, jnp.max(log_w, axis=0, keepdims=True))
            w = jnp.exp(log_w - m_new)
            decay = jnp.exp(g_last + m_prev - m_new)
            wk = w * k
            c_ref[h] = decay * c_prev + _dot_tn(wk.astype(BF16), v)
            n_ref[h] = decay * n_prev + jnp.sum(wk, axis=0, keepdims=True)
            m_ref[h] = m_new
        return carry

    lax.fori_loop(0, ts // CHUNK, chunk, 0)


def _mlstm_branch(proj, gates, gates_t, conv_w, conv_b, *, batch, bw, fh, ts):
    t = proj.shape[0]
    s = t // batch
    ts = _tile(s, ts, CHUNK)
    ns = s // ts
    mh = bw // MLSTM_V_DIM
    row = lambda b, i: b * ns + i
    full2 = lambda b, i: (0, 0)
    kern = functools.partial(_mlstm_kernel, ts=ts, mh=mh, gate_lane=fh)
    return pl.pallas_call(
        kern,
        out_shape=jax.ShapeDtypeStruct((t, bw), BF16),
        grid=(batch, ns),
        in_specs=[pl.BlockSpec((ts, bw), lambda b, i: (row(b, i), PROJ_MQK)),
                  pl.BlockSpec((ts, bw), lambda b, i: (row(b, i), PROJ_MV)),
                  pl.BlockSpec((ts, bw), lambda b, i: (row(b, i), PROJ_MO)),
                  pl.BlockSpec((ts, LANES), lambda b, i: (row(b, i), 0)),
                  pl.BlockSpec((ts // CHUNK, LANES, CHUNK), lambda b, i: (row(b, i), 0, 0)),
                  pl.BlockSpec((CONV_WIDTH, bw), full2),
                  pl.BlockSpec((1, bw), full2)],
        out_specs=pl.BlockSpec((ts, bw), lambda b, i: (row(b, i), 0)),
        scratch_shapes=[pltpu.VMEM((ts + SUBLANES, bw), F32), pltpu.VMEM((ts, bw), F32),
                        pltpu.VMEM((mh, HEAD_DIM, MLSTM_V_DIM), F32),
                        pltpu.VMEM((mh, 1, HEAD_DIM), F32), pltpu.VMEM((mh, 1, 1), F32)],
        compiler_params=_cparams("parallel", "arbitrary"),
        name="mlstm_branch",
    )(proj, proj, proj, gates, gates_t, conv_w, conv_b)


def _merge_kernel(x_ref, y0_ref, y1_ref, y2_ref, wg_ref, bg_ref, wb_ref, o_ref, acc_ref):
    b = pl.program_id(2)
    gate = _sigmoid(_dot(x_ref[...], wg_ref[...]) + bg_ref[...])

    def contrib(y_ref):
        return gate * _dot(y_ref[...], wb_ref[...])

    @pl.when(b == 0)
    def _():
        acc_ref[...] = contrib(y0_ref)

    @pl.when(b == 1)
    def _():
        acc_ref[...] += contrib(y1_ref)

    @pl.when(b == 2)
    def _():
        o_ref[...] = (acc_ref[...] + contrib(y2_ref)).astype(o_ref.dtype)


def _merge(x16, ys, wg, bg, wb, *, tm, tn):
    t, d = x16.shape
    bw = ys[0].shape[1]
    tm, tn = _tile(t, tm, SUBLANES), _tile(d, tn, LANES)
    rows = lambda i, j, b: (i, 0)
    return pl.pallas_call(
        _merge_kernel,
        out_shape=jax.ShapeDtypeStruct((t, d), BF16),
        grid=(t // tm, d // tn, 3),
        in_specs=[_resident_spec((tm, d), rows),
                  _resident_spec((tm, bw), rows), _resident_spec((tm, bw), rows),
                  _resident_spec((tm, bw), rows),
                  pl.BlockSpec((None, d, tn), lambda i, j, b: (b, 0, j)),
                  pl.BlockSpec((None, 1, tn), lambda i, j, b: (b, 0, j)),
                  pl.BlockSpec((None, bw, tn), lambda i, j, b: (b, 0, j))],
        out_specs=pl.BlockSpec((tm, tn), lambda i, j, b: (i, j)),
        scratch_shapes=[pltpu.VMEM((tm, tn), F32)],
        compiler_params=_cparams("parallel", "parallel", "arbitrary"),
        name="merge_branches",
    )(x16, ys[0], ys[1], ys[2], wg, bg, wb)


def _ln_kernel(r_ref, d_ref, g_ref, b_ref, o_ref, o16_ref, *, alpha):
    x = alpha * r_ref[...] + d_ref[...]
    mu = jnp.mean(x, axis=-1, keepdims=True)
    xc = x - mu
    var = jnp.mean(xc * xc, axis=-1, keepdims=True)
    y = xc * lax.rsqrt(var + LN_EPS) * g_ref[...] + b_ref[...]
    o_ref[...] = y
    o16_ref[...] = y.astype(BF16)


def _post_norm(resid, delta, g, b, *, alpha, tr):
    t, d = resid.shape
    tr = _tile(t, tr, SUBLANES)
    rows = lambda i: (i, 0)
    return pl.pallas_call(
        functools.partial(_ln_kernel, alpha=alpha),
        out_shape=(jax.ShapeDtypeStruct((t, d), F32), jax.ShapeDtypeStruct((t, d), BF16)),
        grid=(t // tr,),
        in_specs=[pl.BlockSpec((tr, d), rows), pl.BlockSpec((tr, d), rows),
                  pl.BlockSpec((1, d), lambda i: (0, 0)), pl.BlockSpec((1, d), lambda i: (0, 0))],
        out_specs=(pl.BlockSpec((tr, d), rows), pl.BlockSpec((tr, d), rows)),
        compiler_params=_cparams("parallel"),
        name="post_norm",
    )(resid, delta, g, b)


def _topk_rows(arrays, k):
    n = arrays[0].shape[1]
    kid = lax.broadcasted_iota(jnp.int32, (k, n), 0)
    rids = [lax.broadcasted_iota(jnp.int32, a.shape, 0) for a in arrays]

    def step(i, carry):
        out = []
        for (a, vals, idx), rid in zip(carry, rids):
            m = jnp.max(a, axis=0, keepdims=True)
            first = jnp.min(jnp.where(a == m, rid, a.shape[0]), axis=0, keepdims=True)
            vals = jnp.where(kid == i, m, vals)
            idx = jnp.where(kid == i, first, idx)
            a = jnp.where(rid == first, NEG_INF, a)
            out.append((a, vals, idx))
        return tuple(out)

    init = tuple((a, jnp.zeros((k, n), F32), jnp.zeros((k, n), jnp.int32)) for a in arrays)
    res = lax.fori_loop(0, k, step, init)
    return [(vals, idx) for _, vals, idx in res]


def _pair_candidates(v1, v2):
    k, n = v1.shape
    assert k % SUBLANES == 0
    groups, table = [], []
    neg = jnp.full((SUBLANES, n), NEG_INF, F32)
    sub = lax.broadcasted_iota(jnp.int32, (SUBLANES, n), 0)
    a = 0
    while a < k:
        nb = k // (a + 1)
        if nb > 1:
            for b0 in range(0, nb, SUBLANES):
                g = v1[a:a + 1, :] + v2[b0:b0 + SUBLANES, :]
                groups.append(g if nb - b0 >= SUBLANES else jnp.where(sub < nb - b0, g, neg))
                table.append((a, 0, b0, 1))
            a += 1
        else:
            assert a % SUBLANES == 0
            groups.append(v1[a:a + SUBLANES, :] + v2[0:1, :])
            table.append((a, 1, 0, 0))
            a += SUBLANES
    return jnp.concatenate(groups, axis=0), table


def _pair_of_row(table, pos):
    g = jnp.right_shift(pos, SUBLANES.bit_length() - 1)
    r = jnp.bitwise_and(pos, SUBLANES - 1)
    a = jnp.zeros(pos.shape, jnp.int32)
    b = jnp.zeros(pos.shape, jnp.int32)
    for gi, (a0, da, b0, db) in enumerate(table):
        hit = g == gi
        a = jnp.where(hit, a0 + da * r, a)
        b = jnp.where(hit, b0 + db * r, b)
    return a, b


def _take_rows(table, sel):
    k, n = table.shape
    out = jnp.zeros((k, n), jnp.int32)
    for a in range(k):
        out = jnp.where(sel == a, table[a:a + 1, :], out)
    return out


def _route_kernel(q_ref, keys_ref, i_ref, j_ref, g_ref):
    q = q_ref[...]
    s1 = _dot_nt(keys_ref[0], q[:, :PEER_HALF])
    s2 = _dot_nt(keys_ref[1], q[:, PEER_HALF:])
    (v1, i1), (v2, i2) = _topk_rows([s1, s2], PEER_TOPK)
    cand, table = _pair_candidates(v1, v2)
    ((top, pos),) = _topk_rows([cand], PEER_TOPK)
    e = jnp.exp(top - top[0:1, :])
    g_ref[...] = e / jnp.sum(e, axis=0, keepdims=True)
    a, b = _pair_of_row(table, pos)
    i_ref[...] = _take_rows(i1, a)
    j_ref[...] = _take_rows(i2, b)


def _peer_route(q16, keys16, *, tt):
    t = q16.shape[0]
    ph = q16.shape[1] // (2 * PEER_HALF)
    tt = _tile(t, tt, LANES)
    n = ph * PEER_TOPK
    out = jax.ShapeDtypeStruct((n, t), jnp.int32)
    ospec = pl.BlockSpec((PEER_TOPK, tt), lambda i, h: (h, i))
    return pl.pallas_call(
        _route_kernel,
        out_shape=(out, out, jax.ShapeDtypeStruct((n, t), F32)),
        grid=(t // tt, ph),
        in_specs=[pl.BlockSpec((tt, 2 * PEER_HALF), lambda i, h: (i, h)),
                  pl.BlockSpec((2, PEER_KEYS, PEER_HALF), lambda i, h: (0, 0, 0))],
        out_specs=(ospec, ospec, ospec),
        compiler_params=_cparams("parallel", "parallel"),
        name="peer_route",
    )(q16, keys16)


def _gates_kernel(i_ref, j_ref, g_ref, o_ref, scr_ref, *, tb, pitch):
    n = i_ref.shape[1]
    sub = lax.broadcasted_iota(jnp.int32, (PEER_KEYS, n), 0)

    def token(t, c):
        irow = i_ref[pl.ds(t, 1), :]
        jrow = j_ref[pl.ds(t, 1), :]
        grow = g_ref[pl.ds(t, 1), :]
        a = jnp.where(sub == irow, grow, 0.0).astype(BF16)
        b = jnp.where(sub == jrow, 1.0, 0.0).astype(BF16)
        scr_ref[pl.ds(t, PEER_KEYS, stride=pitch), :] = _dot_nt(a, b)
        return c

    lax.fori_loop(0, tb, token, 0, unroll=GATE_TOKEN_UNROLL)
    for i in range(PEER_KEYS):
        o_ref[:, i * PEER_KEYS:(i + 1) * PEER_KEYS] = (
            scr_ref[i * pitch:i * pitch + tb, :].astype(o_ref.dtype))


def _peer_gates(sel_i, sel_j, gates, *, tb):
    t, n = sel_i.shape
    tb = _tile(t, tb, SUBLANES)
    e = PEER_KEYS * PEER_KEYS
    rows = lambda i: (i, 0)
    pitch = tb + SUBLANES if (tb // SUBLANES) % 2 == 0 else tb
    return pl.pallas_call(
        functools.partial(_gates_kernel, tb=tb, pitch=pitch),
        out_shape=jax.ShapeDtypeStruct((t, e), BF16),
        grid=(t // tb,),
        in_specs=[pl.BlockSpec((tb, n), rows), pl.BlockSpec((tb, n), rows), pl.BlockSpec((tb, n), rows)],
        out_specs=pl.BlockSpec((tb, e), rows),
        scratch_shapes=[pltpu.VMEM((pitch * PEER_KEYS, PEER_KEYS), F32)],
        compiler_params=_cparams("parallel"),
        name="peer_gates",
    )(sel_i, sel_j, gates)


def _expert_kernel(x_ref, u_ref, v_ref, w_ref, o_hbm, acc_ref, sem, *, tc, tt):
    i, j = pl.program_id(0), pl.program_id(1)

    @pl.when(j == 0)
    def _():
        acc_ref[...] = jnp.zeros_like(acc_ref)

    act = _gelu_tanh(_dot_nt(x_ref[...], u_ref[...]))
    p = (w_ref[...].astype(F32) * act).astype(BF16)
    for c in range(0, acc_ref.shape[1], tc):
        acc_ref[:, c:c + tc] += _dot(p, v_ref[:, c:c + tc])

    @pl.when(j == pl.num_programs(1) - 1)
    def _():
        copy = pltpu.make_async_copy(acc_ref, o_hbm.at[pl.ds(i * tt, tt), :], sem)
        copy.start()
        copy.wait()


def _peer_experts(x16, u16, v16, w16, *, tt, te):
    t, d = x16.shape
    e = u16.shape[0]
    tt, te = _tile(t, tt, SUBLANES), _tile(e, te, LANES)
    return pl.pallas_call(
        functools.partial(_expert_kernel, tc=_tile(d, 1024, LANES), tt=tt),
        out_shape=jax.ShapeDtypeStruct((t, d), F32),
        grid=(t // tt, e // te),
        in_specs=[_resident_spec((tt, d), lambda i, j: (i, 0)),
                  pl.BlockSpec((te, d), lambda i, j: (j, 0)),
                  pl.BlockSpec((te, d), lambda i, j: (j, 0)),
                  pl.BlockSpec((tt, te), lambda i, j: (i, j))],
        out_specs=pl.BlockSpec(memory_space=pl.ANY),
        scratch_shapes=[pltpu.VMEM((tt, d), F32), pltpu.SemaphoreType.DMA],
        compiler_params=_cparams("arbitrary", "arbitrary"),
        name="peer_experts",
    )(x16, u16, v16, w16)


def _mixer(h16, batch, w_in, fox_f_bias, lru_conv_w, lru_conv_b, lru_gate_a_w, lru_gate_a_b,
           lru_gate_x_w, lru_gate_x_b, lru_lambda, mlstm_conv_w, mlstm_conv_b, mlstm_i_bias,
           mlstm_f_bias, w_branch, w_merge_gate, b_merge_gate, w_out):
    t, d = h16.shape
    bw = w_branch.shape[1]
    fh = bw // HEAD_DIM
    mh = bw // MLSTM_V_DIM
    s = t // batch

    o_ff = 3 * bw
    o_lx = o_ff + fh
    o_mi = o_lx + 5 * bw
    q_scale = HEAD_DIM ** -0.5 * LOG2E
    w_qv_t = jnp.concatenate([w_in[:, :bw] * q_scale, w_in[:, 2 * bw:o_ff]], axis=1).T.astype(BF16)
    w_main = jnp.concatenate([w_in[:, bw:2 * bw], w_in[:, o_lx:o_mi]], axis=1).astype(BF16)
    n_gate = fh + 2 * mh
    w_gate = jnp.concatenate([w_in[:, o_ff:o_lx], w_in[:, o_mi:]], axis=1)
    w_gate = jnp.pad(w_gate, ((0, 0), (0, LANES - n_gate))).astype(BF16)
    gate_bias = jnp.pad(jnp.concatenate([fox_f_bias, mlstm_i_bias, mlstm_f_bias]),
                        (0, LANES - n_gate)).reshape(1, LANES)

    proj = _matmul(h16, w_main, BF16, tm=1024, tn=1024, name="in_proj")
    gate_z = _matmul(h16, w_gate, F32, tm=1024, tn=LANES, name="gate_proj")
    gates = _gate_prep(gate_z, gate_bias, batch=batch, fh=fh, mh=mh)
    qv_t = _matmul_t(h16, w_qv_t, BF16, tm=1024, tn=1024, name="qv_proj_t")
    cum = gates.T[:fh]
    cum_rep = jnp.broadcast_to(cum[:, :, None], (fh, t, LANES))
    y_fox = _fox_attention(qv_t, proj, cum_rep, cum.reshape(fh, 1, t), batch=batch, fh=fh, tq=512)
    y_lru = _lru_branch(proj, lru_conv_w, lru_conv_b.reshape(1, bw), lru_gate_a_w.astype(BF16),
                        lru_gate_a_b.reshape(1, bw), lru_gate_x_w.astype(BF16),
                        lru_gate_x_b.reshape(1, bw), lru_lambda.reshape(1, bw),
                        batch=batch, bw=bw, ts=256)
    gates_ct = gates.reshape(t // CHUNK, CHUNK, LANES).transpose(0, 2, 1)
    y_mlstm = _mlstm_branch(proj, gates, gates_ct, mlstm_conv_w, mlstm_conv_b.reshape(1, bw),
                            batch=batch, bw=bw, fh=fh, ts=512)
    merged = _merge(h16, (y_fox, y_lru, y_mlstm), w_merge_gate.astype(BF16),
                    b_merge_gate.reshape(3, 1, d), w_branch.astype(BF16), tm=1024, tn=512)
    return _matmul(merged, w_out.astype(BF16), F32, tm=1024, tn=1024, name="out_proj")


def _peer(h16, w_query, sub_keys, expert_u, expert_v):
    q16 = _matmul(h16, w_query.astype(BF16), BF16, tm=1024, tn=1024, name="peer_query")
    sel_i, sel_j, gates = _peer_route(q16, sub_keys.astype(BF16), tt=512)
    w16 = _peer_gates(sel_i.T, sel_j.T, gates.T, tb=128)
    return _peer_experts(h16, expert_u.astype(BF16), expert_v.astype(BF16), w16, tt=1024, te=512)


def kernel(x, w_in, fox_f_bias, lru_conv_w, lru_conv_b, lru_gate_a_w, lru_gate_a_b, lru_gate_x_w, lru_gate_x_b, lru_lambda, mlstm_conv_w, mlstm_conv_b, mlstm_i_bias, mlstm_f_bias, w_branch, w_merge_gate, b_merge_gate, w_out, ln1_g, ln1_b, peer_w_query, peer_sub_keys, peer_u, peer_v, ln2_g, ln2_b):
    batch, seq, d = x.shape
    depth = w_in.shape[0]
    alpha = (2.0 * depth) ** 0.25
    h = x.reshape(batch * seq, d)
    h16 = h.astype(BF16)
    for l in range(depth):
        mix = _mixer(h16, batch, w_in[l], fox_f_bias[l], lru_conv_w[l], lru_conv_b[l], lru_gate_a_w[l],
                     lru_gate_a_b[l], lru_gate_x_w[l], lru_gate_x_b[l], lru_lambda[l], mlstm_conv_w[l],
                     mlstm_conv_b[l], mlstm_i_bias[l], mlstm_f_bias[l], w_branch[l], w_merge_gate[l],
                     b_merge_gate[l], w_out[l])
        h, h16 = _post_norm(h, mix, ln1_g[l].reshape(1, d), ln1_b[l].reshape(1, d), alpha=alpha, tr=256)
        ffn = _peer(h16, peer_w_query[l], peer_sub_keys[l], peer_u[l], peer_v[l])
        h, h16 = _post_norm(h, ffn, ln2_g[l].reshape(1, d), ln2_b[l].reshape(1, d), alpha=alpha, tr=256)
    return h.reshape(batch, seq, d)
```

```python
import functools
import math

import jax
import jax.numpy as jnp
from jax import lax
from jax.experimental import pallas as pl
from jax.experimental.pallas import tpu as pltpu

F32 = jnp.float32
BF16 = jnp.bfloat16

HEAD_DIM = 128
MLSTM_V_DIM = 256
CHUNK = 64
CONV_WIDTH = 4
LRU_C = 8.0
PEER_KEYS = 128
PEER_HALF = 128
PEER_TOPK = 16
LN_EPS = 1e-5
GATE_TOKEN_UNROLL = 16

V7X_VMEM_BYTES = 64 * 1024 * 1024
VMEM_LIMIT = V7X_VMEM_BYTES - 8 * 1024 * 1024
LANES = 128
SUBLANES = 8

NEG_INF = float("-inf")
LOG2E = math.log2(math.e)

PROJ_FK, PROJ_LX, PROJ_LG, PROJ_MQK, PROJ_MV, PROJ_MO = range(6)


def _tile(dim, pref, align):
    if dim <= pref:
        return dim
    t = pref - pref % align
    while dim % t:
        t -= align
    assert t > 0, (dim, pref, align)
    return t


def _layer_spec(layer, block_shape, index_map):
    return pl.BlockSpec((None,) + tuple(block_shape), lambda *g: (layer,) + tuple(index_map(*g)))


def _resident_spec(block_shape, index_map):
    return pl.BlockSpec(block_shape, index_map, pipeline_mode=pl.Buffered(1))


def _cparams(*sem):
    return pltpu.CompilerParams(dimension_semantics=sem, vmem_limit_bytes=VMEM_LIMIT)


def _log_sigmoid(x):
    return jnp.minimum(x, 0.0) - jnp.log1p(jnp.exp(-jnp.abs(x)))


def _sigmoid(x):
    return 1.0 / (1.0 + jnp.exp(-x))


def _gelu_tanh(x):
    c = math.sqrt(2.0 / math.pi)
    return 0.5 * x * (1.0 + jnp.tanh(c * (x + 0.044715 * (x * x * x))))


def _expm1(z):
    u = jnp.exp(z)
    um1 = u - 1.0
    near = um1 * z / jnp.log(u)
    return jnp.where(jnp.abs(z) > 0.5, um1, jnp.where(u == 1.0, z, near))


def _split3(x):
    hi = x.astype(BF16)
    r1 = x - hi.astype(F32)
    mid = r1.astype(BF16)
    lo = (r1 - mid.astype(F32)).astype(BF16)
    return hi, mid, lo


def _dot(a, b):
    return jnp.dot(a, b, preferred_element_type=F32)


def _dot_nt(a, b):
    return lax.dot_general(a, b, (((1,), (1,)), ((), ())), preferred_element_type=F32)


def _dot_tn(a, b):
    return lax.dot_general(a, b, (((0,), (0,)), ((), ())), preferred_element_type=F32)


def _dot3_left(tri, x):
    hi, mid, lo = _split3(x)
    return _dot(tri, hi) + _dot(tri, mid) + _dot(tri, lo)


def _dot3_right(x, tri):
    hi, mid, lo = _split3(x)
    return _dot(hi, tri) + _dot(mid, tri) + _dot(lo, tri)


def _mm_kernel(a_ref, b_ref, o_ref):
    o_ref[...] = _dot(a_ref[...], b_ref[...]).astype(o_ref.dtype)


def _matmul(a, b, out_dtype, *, layer, tm, tn, name):
    m, k = a.shape
    n = b.shape[2]
    tm, tn = _tile(m, tm, SUBLANES), _tile(n, tn, LANES)
    return pl.pallas_call(
        _mm_kernel,
        out_shape=jax.ShapeDtypeStruct((m, n), out_dtype),
        grid=(m // tm, n // tn),
        in_specs=[pl.BlockSpec((tm, k), lambda i, j: (i, 0)),
                  _layer_spec(layer, (k, tn), lambda i, j: (0, j))],
        out_specs=pl.BlockSpec((tm, tn), lambda i, j: (i, j)),
        compiler_params=_cparams("parallel", "parallel"),
        name=name,
    )(a, b)


def _mm_t_kernel(a_ref, bt_ref, o_ref):
    o_ref[...] = _dot_nt(bt_ref[...], a_ref[...]).astype(o_ref.dtype)


def _matmul_t(a, bt, out_dtype, *, layer, tm, tn, name):
    m, k = a.shape
    n = bt.shape[1]
    tm, tn = _tile(m, tm, SUBLANES), _tile(n, tn, LANES)
    return pl.pallas_call(
        _mm_t_kernel,
        out_shape=jax.ShapeDtypeStruct((n, m), out_dtype),
        grid=(m // tm, n // tn),
        in_specs=[pl.BlockSpec((tm, k), lambda i, j: (i, 0)),
                  _layer_spec(layer, (tn, k), lambda i, j: (j, 0))],
        out_specs=pl.BlockSpec((tn, tm), lambda i, j: (j, i)),
        compiler_params=_cparams("parallel", "parallel"),
        name=name,
    )(a, bt)


def _gate_kernel(z_ref, b_ref, o_ref, carry_ref, *, fh, mh, ts):
    @pl.when(pl.program_id(1) == 0)
    def _():
        carry_ref[...] = jnp.zeros_like(carry_ref)

    z = z_ref[...] + b_ref[...]
    ls = _log_sigmoid(z)
    lane = lax.broadcasted_iota(jnp.int32, z.shape, 1)
    row = lax.broadcasted_iota(jnp.int32, (ts, ts), 0)
    col = lax.broadcasted_iota(jnp.int32, (ts, ts), 1)
    tri = jnp.where(col <= row, 1.0, 0.0).astype(BF16)
    cum = _dot3_left(tri, ls) + carry_ref[...]
    carry_ref[...] = cum[ts - 1:ts, :]
    o_ref[...] = jnp.where(lane < fh, cum, jnp.where(lane < fh + mh, z, ls))


def _gate_prep(z, bias, *, batch, fh, mh):
    t = z.shape[0]
    s = t // batch
    ts = min(256, s)
    ns = s // ts
    return pl.pallas_call(
        functools.partial(_gate_kernel, fh=fh, mh=mh, ts=ts),
        out_shape=jax.ShapeDtypeStruct((t, LANES), F32),
        grid=(batch, ns),
        in_specs=[pl.BlockSpec((ts, LANES), lambda b, i: (b * ns + i, 0)),
                  pl.BlockSpec((1, LANES), lambda b, i: (0, 0))],
        out_specs=pl.BlockSpec((ts, LANES), lambda b, i: (b * ns + i, 0)),
        scratch_shapes=[pltpu.VMEM((1, LANES), F32)],
        compiler_params=_cparams("parallel", "arbitrary"),
        name="gate_prep",
    )(z, bias)


def _fox_kernel(qt_ref, k_ref, vt_ref, ck_ref, cq_ref, o_ref, m_ref, acc_ref, r_ref, kx_ref, *, tq):
    qi = pl.program_id(2)
    dh = HEAD_DIM

    @pl.when(qi == 0)
    def _():
        hi, mid, lo = _split3(ck_ref[...] * LOG2E)
        lane = lax.broadcasted_iota(jnp.int32, hi.shape, 1)
        zero = jnp.zeros_like(hi)
        kx_ref[:, :dh] = k_ref[...]
        kx_ref[:, dh:] = jnp.where(lane == 0, hi, jnp.where(lane == 1, mid, jnp.where(lane == 2, lo, zero)))

    sub = lax.broadcasted_iota(jnp.int32, (dh, tq), 0)
    qx = jnp.concatenate([qt_ref[...], jnp.where(sub < 3, -1.0, 0.0).astype(BF16)], axis=0)
    cq = cq_ref[...] * LOG2E
    ones = jnp.ones((SUBLANES, tq), BF16)
    m_ref[...] = jnp.full_like(m_ref, NEG_INF)
    acc_ref[...] = jnp.zeros_like(acc_ref)

    def scores(kj, slot):
        start = pl.multiple_of(kj * tq, tq)
        r_ref[slot] = _dot(kx_ref[pl.ds(start, tq), :], qx)

    def update(kj, slot, masked):
        start = pl.multiple_of(kj * tq, tq)
        vt1 = jnp.concatenate([vt_ref[:, pl.ds(start, tq)], ones], axis=0)
        r = r_ref[slot]
        if masked:
            key = lax.broadcasted_iota(jnp.int32, (tq, tq), 0)
            qry = lax.broadcasted_iota(jnp.int32, (tq, tq), 1)
            r = jnp.where(key <= qry, r, NEG_INF)
        m_prev = m_ref[...]
        m_new = jnp.maximum(m_prev, jnp.max(r, axis=0, keepdims=True) + cq)
        p = jnp.exp2(r + (cq - m_new))
        alpha = jnp.exp2(m_prev - m_new)
        acc_ref[...] = alpha * acc_ref[...] + _dot(vt1, p.astype(BF16))
        m_ref[...] = m_new

    scores(0, 0)

    def body(pair, c):
        scores(2 * pair + 1, 1)
        update(2 * pair, 0, False)
        scores(2 * pair + 2, 0)
        update(2 * pair + 1, 1, False)
        return c

    lax.fori_loop(0, qi // 2, body, 0)

    @pl.when(qi % 2 == 1)
    def _():
        scores(qi, 1)
        update(qi - 1, 0, False)
        update(qi, 1, True)

    @pl.when(qi % 2 == 0)
    def _():
        update(qi, 0, True)

    o_ref[...] = (acc_ref[:dh, :] / acc_ref[dh:dh + 1, :]).T.astype(o_ref.dtype)


def _fox_attention(qv_t, proj, cum_rep, cum_row, *, batch, fh, tq):
    t = proj.shape[0]
    s = t // batch
    tq = _tile(s, tq, LANES)
    nq = s // tq
    return pl.pallas_call(
        functools.partial(_fox_kernel, tq=tq),
        out_shape=jax.ShapeDtypeStruct((t, fh * HEAD_DIM), BF16),
        grid=(batch, fh, nq),
        in_specs=[pl.BlockSpec((HEAD_DIM, tq), lambda b, h, i: (h, b * nq + i)),
                  pl.BlockSpec((s, HEAD_DIM), lambda b, h, i: (b, h)),
                  pl.BlockSpec((HEAD_DIM, s), lambda b, h, i: (fh + h, b)),
                  pl.BlockSpec((None, s, LANES), lambda b, h, i: (h, b, 0)),
                  pl.BlockSpec((None, 1, tq), lambda b, h, i: (h, 0, b * nq + i))],
        out_specs=pl.BlockSpec((tq, HEAD_DIM), lambda b, h, i: (b * nq + i, h)),
        scratch_shapes=[pltpu.VMEM((1, tq), F32), pltpu.VMEM((HEAD_DIM + SUBLANES, tq), F32),
                        pltpu.VMEM((2, tq, tq), F32), pltpu.VMEM((s, 2 * HEAD_DIM), BF16)],
        compiler_params=_cparams("parallel", "parallel", "arbitrary"),
        name="fox_attention",
    )(qv_t, proj, qv_t, cum_rep, cum_row)


def _causal_conv(x, ext_ref, w_ref, b_ref, ts, first):
    @pl.when(first)
    def _():
        ext_ref[0:SUBLANES, :] = jnp.zeros((SUBLANES, x.shape[1]), F32)

    ext_ref[SUBLANES:SUBLANES + ts, :] = x
    y = b_ref[...] + w_ref[CONV_WIDTH - 1:CONV_WIDTH, :] * x
    for k in range(CONV_WIDTH - 1):
        off = SUBLANES - (CONV_WIDTH - 1) + k
        y = y + w_ref[k:k + 1, :] * ext_ref[off:off + ts, :]
    ext_ref[0:SUBLANES, :] = ext_ref[ts:ts + SUBLANES, :]
    return y


def _lru_kernel(lx_ref, lg_ref, cw_ref, cb_ref, wa_ref, ba_ref, wx_ref, bx_ref, lam_ref, o_ref,
                ext_ref, a_ref, u_ref, h_ref, *, ts, nblk):
    first = pl.program_id(1) == 0

    @pl.when(first)
    def _():
        h_ref[...] = jnp.zeros_like(h_ref)

    xl = _causal_conv(lx_ref[...].astype(F32), ext_ref, cw_ref, cb_ref, ts, first)
    lam = lam_ref[...]
    nsp = jnp.maximum(-lam, 0.0) + jnp.log1p(jnp.exp(-jnp.abs(lam)))
    for n in range(nblk):
        sl = slice(n * HEAD_DIM, (n + 1) * HEAD_DIM)
        xb = xl[:, sl]
        xb16 = xb.astype(BF16)
        r = _sigmoid(_dot(xb16, wa_ref[n]) + ba_ref[:, sl])
        gi = _sigmoid(_dot(xb16, wx_ref[n]) + bx_ref[:, sl])
        log_a = (-LRU_C) * r * nsp[:, sl]
        a_ref[:, sl] = jnp.exp(log_a)
        u_ref[:, sl] = jnp.sqrt(-_expm1(2.0 * log_a)) * gi * xb

    def group(g, h):
        base = pl.multiple_of(g * SUBLANES, SUBLANES)
        a8 = a_ref[pl.ds(base, SUBLANES), :]
        u8 = u_ref[pl.ds(base, SUBLANES), :]
        rows = []
        for r in range(SUBLANES):
            h = a8[r:r + 1, :] * h + u8[r:r + 1, :]
            rows.append(h)
        u_ref[pl.ds(base, SUBLANES), :] = jnp.concatenate(rows, axis=0)
        return h

    h_ref[...] = lax.fori_loop(0, ts // SUBLANES, group, h_ref[...])
    o_ref[...] = (u_ref[...] * _gelu_tanh(lg_ref[...].astype(F32))).astype(o_ref.dtype)


def _lru_branch(proj, conv_w, conv_b, w_a, b_a, w_x, b_x, lam, *, batch, bw, ts):
    t = proj.shape[0]
    s = t // batch
    ts = _tile(s, ts, CHUNK)
    ns = s // ts
    nblk = bw // HEAD_DIM
    row = lambda b, i: b * ns + i
    full2 = lambda b, i: (0, 0)
    full3 = lambda b, i: (0, 0, 0)
    return pl.pallas_call(
        functools.partial(_lru_kernel, ts=ts, nblk=nblk),
        out_shape=jax.ShapeDtypeStruct((t, bw), BF16),
        grid=(batch, ns),
        in_specs=[pl.BlockSpec((ts, bw), lambda b, i: (row(b, i), PROJ_LX)),
                  pl.BlockSpec((ts, bw), lambda b, i: (row(b, i), PROJ_LG)),
                  pl.BlockSpec((CONV_WIDTH, bw), full2),
                  pl.BlockSpec((1, bw), full2),
                  pl.BlockSpec((nblk, HEAD_DIM, HEAD_DIM), full3),
                  pl.BlockSpec((1, bw), full2),
                  pl.BlockSpec((nblk, HEAD_DIM, HEAD_DIM), full3),
                  pl.BlockSpec((1, bw), full2),
                  pl.BlockSpec((1, bw), full2)],
        out_specs=pl.BlockSpec((ts, bw), lambda b, i: (row(b, i), 0)),
        scratch_shapes=[pltpu.VMEM((ts + SUBLANES, bw), F32), pltpu.VMEM((ts, bw), F32),
                        pltpu.VMEM((ts, bw), F32), pltpu.VMEM((1, bw), F32)],
        compiler_params=_cparams("parallel", "arbitrary"),
        name="rglru_branch",
    )(proj, proj, conv_w, conv_b, w_a, b_a, w_x, b_x, lam)


def _mlstm_kernel(qk_ref, v_ref, og_ref, g_ref, gt_ref, cw_ref, cb_ref, o_ref,
                  ext_ref, qk_s, c_ref, n_ref, m_ref, *, ts, mh, gate_lane):
    first = pl.program_id(1) == 0

    @pl.when(first)
    def _():
        c_ref[...] = jnp.zeros_like(c_ref)
        n_ref[...] = jnp.zeros_like(n_ref)
        m_ref[...] = jnp.zeros_like(m_ref)

    conv = _causal_conv(qk_ref[...].astype(F32), ext_ref, cw_ref, cb_ref, ts, first)
    qk_s[...] = conv * _sigmoid(conv)
    mqk = mh * HEAD_DIM
    kscale = HEAD_DIM ** -0.5

    row = lax.broadcasted_iota(jnp.int32, (CHUNK, CHUNK), 0)
    col = lax.broadcasted_iota(jnp.int32, (CHUNK, CHUNK), 1)
    tril = col <= row
    tri_lo = jnp.where(tril, 1.0, 0.0).astype(BF16)
    tri_up = jnp.where(row <= col, 1.0, 0.0).astype(BF16)

    def chunk(c, carry):
        r0 = pl.multiple_of(c * CHUNK, CHUNK)
        gates = g_ref[pl.ds(r0, CHUNK), :]
        gates_t = gt_ref[c]
        cum_c = _dot3_left(tri_lo, gates)
        cum_r = _dot3_right(gates_t, tri_up)
        for h in range(mh):
            li, lf = gate_lane + h, gate_lane + mh + h
            g_col = cum_c[:, lf:lf + 1]
            g_row = cum_r[lf:lf + 1, :]
            ig_col = gates[:, li:li + 1]
            ig_row = gates_t[li:li + 1, :]
            q = qk_s[pl.ds(r0, CHUNK), h * HEAD_DIM:(h + 1) * HEAD_DIM]
            k = qk_s[pl.ds(r0, CHUNK), mqk + h * HEAD_DIM:mqk + (h + 1) * HEAD_DIM] * kscale
            v = v_ref[pl.ds(r0, CHUNK), h * MLSTM_V_DIM:(h + 1) * MLSTM_V_DIM]
            q16 = q.astype(BF16)
            m_prev = m_ref[h]
            c_prev = c_ref[h]
            n_prev = n_ref[h]

            log_d = jnp.where(tril, g_col - g_row + ig_row, NEG_INF)
            m_inter = g_col + m_prev
            m_row = jnp.maximum(m_inter, jnp.max(log_d, axis=1, keepdims=True))
            sc = _dot_nt(q16, k.astype(BF16)) * jnp.exp(log_d - m_row)
            inter = jnp.exp(m_inter - m_row)
            num = _dot(sc.astype(BF16), v) + inter * _dot(q16, c_prev.astype(BF16))
            den = jnp.sum(sc, axis=1, keepdims=True) + inter * jnp.sum(q * n_prev, axis=1, keepdims=True)
            hh = num / jnp.maximum(jnp.abs(den), jnp.exp(-m_row))
            og = og_ref[pl.ds(r0, CHUNK), h * MLSTM_V_DIM:(h + 1) * MLSTM_V_DIM].astype(F32)
            o_ref[pl.ds(r0, CHUNK), h * MLSTM_V_DIM:(h + 1) * MLSTM_V_DIM] = (
                _sigmoid(og) * hh).astype(o_ref.dtype)

            g_last = g_col[CHUNK - 1:CHUNK, :]
            log_w = g_last - g_col + ig_col
            m_new = jnp.maximum(g_last + m_prev, jnp.max(log_w, axis=0, keepdims=True))
            w = jnp.exp(log_w - m_new)
            decay = jnp.exp(g_last + m_prev - m_new)
            wk = w * k
            c_ref[h] = decay * c_prev + _dot_tn(wk.astype(BF16), v)
            n_ref[h] = decay * n_prev + jnp.sum(wk, axis=0, keepdims=True)
            m_ref[h] = m_new
        return carry

    lax.fori_loop(0, ts // CHUNK, chunk, 0)


def _mlstm_branch(proj, gates, gates_t, conv_w, conv_b, *, batch, bw, fh, ts):
    t = proj.shape[0]
    s = t // batch
    ts = _tile(s, ts, CHUNK)
    ns = s // ts
    mh = bw // MLSTM_V_DIM
    row = lambda b, i: b * ns + i
    full2 = lambda b, i: (0, 0)
    kern = functools.partial(_mlstm_kernel, ts=ts, mh=mh, gate_lane=fh)
    return pl.pallas_call(
        kern,
        out_shape=jax.ShapeDtypeStruct((t, bw), BF16),
        grid=(batch, ns),
        in_specs=[pl.BlockSpec((ts, bw), lambda b, i: (row(b, i), PROJ_MQK)),
                  pl.BlockSpec((ts, bw), lambda b, i: (row(b, i), PROJ_MV)),
                  pl.BlockSpec((ts, bw), lambda b, i: (row(b, i), PROJ_MO)),
                  pl.BlockSpec((ts, LANES), lambda b, i: (row(b, i), 0)),
                  pl.BlockSpec((ts // CHUNK, LANES, CHUNK), lambda b, i: (row(b, i), 0, 0)),
                  pl.BlockSpec((CONV_WIDTH, bw), full2),
                  pl.BlockSpec((1, bw), full2)],
        out_specs=pl.BlockSpec((ts, bw), lambda b, i: (row(b, i), 0)),
        scratch_shapes=[pltpu.VMEM((ts + SUBLANES, bw), F32), pltpu.VMEM((ts, bw), F32),
                        pltpu.VMEM((mh, HEAD_DIM, MLSTM_V_DIM), F32),
                        pltpu.VMEM((mh, 1, HEAD_DIM), F32), pltpu.VMEM((mh, 1, 1), F32)],
        compiler_params=_cparams("parallel", "arbitrary"),
        name="mlstm_branch",
    )(proj, proj, proj, gates, gates_t, conv_w, conv_b)


def _merge_kernel(x_ref, y0_ref, y1_ref, y2_ref, wg_ref, bg_ref, wb_ref, o_ref, acc_ref):
    b = pl.program_id(2)
    gate = _sigmoid(_dot(x_ref[...], wg_ref[...]) + bg_ref[...])

    def contrib(y_ref):
        return gate * _dot(y_ref[...], wb_ref[...])

    @pl.when(b == 0)
    def _():
        acc_ref[...] = contrib(y0_ref)

    @pl.when(b == 1)
    def _():
        acc_ref[...] += contrib(y1_ref)

    @pl.when(b == 2)
    def _():
        o_ref[...] = (acc_ref[...] + contrib(y2_ref)).astype(o_ref.dtype)


def _merge(x16, ys, wg, bg, wb, *, layer, tm, tn):
    t, d = x16.shape
    bw = ys[0].shape[1]
    tm, tn = _tile(t, tm, SUBLANES), _tile(d, tn, LANES)
    rows = lambda i, j, b: (i, 0)
    return pl.pallas_call(
        _merge_kernel,
        out_shape=jax.ShapeDtypeStruct((t, d), BF16),
        grid=(t // tm, d // tn, 3),
        in_specs=[_resident_spec((tm, d), rows),
                  _resident_spec((tm, bw), rows), _resident_spec((tm, bw), rows),
                  _resident_spec((tm, bw), rows),
                  _layer_spec(layer, (None, d, tn), lambda i, j, b: (b, 0, j)),
                  _layer_spec(layer, (None, 1, tn), lambda i, j, b: (b, 0, j)),
                  _layer_spec(layer, (None, bw, tn), lambda i, j, b: (b, 0, j))],
        out_specs=pl.BlockSpec((tm, tn), lambda i, j, b: (i, j)),
        scratch_shapes=[pltpu.VMEM((tm, tn), F32)],
        compiler_params=_cparams("parallel", "parallel", "arbitrary"),
        name="merge_branches",
    )(x16, ys[0], ys[1], ys[2], wg, bg, wb)


def _ln_kernel(r_ref, d_ref, g_ref, b_ref, o_ref, o16_ref, *, alpha):
    x = alpha * r_ref[...] + d_ref[...]
    mu = jnp.mean(x, axis=-1, keepdims=True)
    xc = x - mu
    var = jnp.mean(xc * xc, axis=-1, keepdims=True)
    y = xc * lax.rsqrt(var + LN_EPS) * g_ref[...] + b_ref[...]
    o_ref[...] = y
    o16_ref[...] = y.astype(BF16)


def _post_norm(resid, delta, g, b, *, alpha, tr):
    t, d = resid.shape
    tr = _tile(t, tr, SUBLANES)
    rows = lambda i: (i, 0)
    return pl.pallas_call(
        functools.partial(_ln_kernel, alpha=alpha),
        out_shape=(jax.ShapeDtypeStruct((t, d), F32), jax.ShapeDtypeStruct((t, d), BF16)),
        grid=(t // tr,),
        in_specs=[pl.BlockSpec((tr, d), rows), pl.BlockSpec((tr, d), rows),
                  pl.BlockSpec((1, d), lambda i: (0, 0)), pl.BlockSpec((1, d), lambda i: (0, 0))],
        out_specs=(pl.BlockSpec((tr, d), rows), pl.BlockSpec((tr, d), rows)),
        compiler_params=_cparams("parallel"),
        name="post_norm",
    )(resid, delta, g, b)


def _topk_rows(arrays, k):
    n = arrays[0].shape[1]
    kid = lax.broadcasted_iota(jnp.int32, (k, n), 0)
    rids = [lax.broadcasted_iota(jnp.int32, a.shape, 0) for a in arrays]

    def step(i, carry):
        out = []
        for (a, vals, idx), rid in zip(carry, rids):
            m = jnp.max(a, axis=0, keepdims=True)
            first = jnp.min(jnp.where(a == m, rid, a.shape[0]), axis=0, keepdims=True)
            vals = jnp.where(kid == i, m, vals)
            idx = jnp.where(kid == i, first, idx)
            a = jnp.where(rid == first, NEG_INF, a)
            out.append((a, vals, idx))
        return tuple(out)

    init = tuple((a, jnp.zeros((k, n), F32), jnp.zeros((k, n), jnp.int32)) for a in arrays)
    res = lax.fori_loop(0, k, step, init)
    return [(vals, idx) for _, vals, idx in res]


def _pair_candidates(v1, v2):
    k, n = v1.shape
    assert k % SUBLANES == 0
    groups, table = [], []
    neg = jnp.full((SUBLANES, n), NEG_INF, F32)
    sub = lax.broadcasted_iota(jnp.int32, (SUBLANES, n), 0)
    a = 0
    while a < k:
        nb = k // (a + 1)
        if nb > 1:
            for b0 in range(0, nb, SUBLANES):
                g = v1[a:a + 1, :] + v2[b0:b0 + SUBLANES, :]
                groups.append(g if nb - b0 >= SUBLANES else jnp.where(sub < nb - b0, g, neg))
                table.append((a, 0, b0, 1))
            a += 1
        else:
            assert a % SUBLANES == 0
            groups.append(v1[a:a + SUBLANES, :] + v2[0:1, :])
            table.append((a, 1, 0, 0))
            a += SUBLANES
    return jnp.concatenate(groups, axis=0), table


def _pair_of_row(table, pos):
    g = jnp.right_shift(pos, SUBLANES.bit_length() - 1)
    r = jnp.bitwise_and(pos, SUBLANES - 1)
    a = jnp.zeros(pos.shape, jnp.int32)
    b = jnp.zeros(pos.shape, jnp.int32)
    for gi, (a0, da, b0, db) in enumerate(table):
        hit = g == gi
        a = jnp.where(hit, a0 + da * r, a)
        b = jnp.where(hit, b0 + db * r, b)
    return a, b


def _take_rows(table, sel):
    k, n = table.shape
    out = jnp.zeros((k, n), jnp.int32)
    for a in range(k):
        out = jnp.where(sel == a, table[a:a + 1, :], out)
    return out


def _route_kernel(q_ref, keys_ref, i_ref, j_ref, g_ref):
    q = q_ref[...]
    s1 = _dot_nt(keys_ref[0], q[:, :PEER_HALF])
    s2 = _dot_nt(keys_ref[1], q[:, PEER_HALF:])
    (v1, i1), (v2, i2) = _topk_rows([s1, s2], PEER_TOPK)
    cand, table = _pair_candidates(v1, v2)
    ((top, pos),) = _topk_rows([cand], PEER_TOPK)
    e = jnp.exp(top - top[0:1, :])
    g_ref[...] = e / jnp.sum(e, axis=0, keepdims=True)
    a, b = _pair_of_row(table, pos)
    i_ref[...] = _take_rows(i1, a)
    j_ref[...] = _take_rows(i2, b)


def _peer_route(q16, keys16, *, tt):
    t = q16.shape[0]
    ph = q16.shape[1] // (2 * PEER_HALF)
    tt = _tile(t, tt, LANES)
    n = ph * PEER_TOPK
    out = jax.ShapeDtypeStruct((n, t), jnp.int32)
    ospec = pl.BlockSpec((PEER_TOPK, tt), lambda i, h: (h, i))
    return pl.pallas_call(
        _route_kernel,
        out_shape=(out, out, jax.ShapeDtypeStruct((n, t), F32)),
        grid=(t // tt, ph),
        in_specs=[pl.BlockSpec((tt, 2 * PEER_HALF), lambda i, h: (i, h)),
                  pl.BlockSpec((2, PEER_KEYS, PEER_HALF), lambda i, h: (0, 0, 0))],
        out_specs=(ospec, ospec, ospec),
        compiler_params=_cparams("parallel", "parallel"),
        name="peer_route",
    )(q16, keys16)


def _gates_kernel(i_ref, j_ref, g_ref, o_ref, scr_ref, *, tb, pitch):
    n = i_ref.shape[1]
    sub = lax.broadcasted_iota(jnp.int32, (PEER_KEYS, n), 0)

    def token(t, c):
        irow = i_ref[pl.ds(t, 1), :]
        jrow = j_ref[pl.ds(t, 1), :]
        grow = g_ref[pl.ds(t, 1), :]
        a = jnp.where(sub == irow, grow, 0.0).astype(BF16)
        b = jnp.where(sub == jrow, 1.0, 0.0).astype(BF16)
        scr_ref[pl.ds(t, PEER_KEYS, stride=pitch), :] = _dot_nt(a, b)
        return c

    lax.fori_loop(0, tb, token, 0, unroll=GATE_TOKEN_UNROLL)
    for i in range(PEER_KEYS):
        o_ref[:, i * PEER_KEYS:(i + 1) * PEER_KEYS] = (
            scr_ref[i * pitch:i * pitch + tb, :].astype(o_ref.dtype))


def _peer_gates(sel_i, sel_j, gates, *, tb):
    t, n = sel_i.shape
    tb = _tile(t, tb, SUBLANES)
    e = PEER_KEYS * PEER_KEYS
    rows = lambda i: (i, 0)
    pitch = tb + SUBLANES if (tb // SUBLANES) % 2 == 0 else tb
    return pl.pallas_call(
        functools.partial(_gates_kernel, tb=tb, pitch=pitch),
        out_shape=jax.ShapeDtypeStruct((t, e), BF16),
        grid=(t // tb,),
        in_specs=[pl.BlockSpec((tb, n), rows), pl.BlockSpec((tb, n), rows), pl.BlockSpec((tb, n), rows)],
        out_specs=pl.BlockSpec((tb, e), rows),
        scratch_shapes=[pltpu.VMEM((pitch * PEER_KEYS, PEER_KEYS), F32)],
        compiler_params=_cparams("parallel"),
        name="peer_gates",
    )(sel_i, sel_j, gates)


def _expert_kernel(x_ref, u_ref, v_ref, w_ref, o_hbm, acc_ref, sem, *, tc, tt):
    i, j = pl.program_id(0), pl.program_id(1)

    @pl.when(j == 0)
    def _():
        acc_ref[...] = jnp.zeros_like(acc_ref)

    act = _gelu_tanh(_dot_nt(x_ref[...], u_ref[...]))
    p = (w_ref[...].astype(F32) * act).astype(BF16)
    for c in range(0, acc_ref.shape[1], tc):
        acc_ref[:, c:c + tc] += _dot(p, v_ref[:, c:c + tc])

    @pl.when(j == pl.num_programs(1) - 1)
    def _():
        copy = pltpu.make_async_copy(acc_ref, o_hbm.at[pl.ds(i * tt, tt), :], sem)
        copy.start()
        copy.wait()


def _peer_experts(x16, u16, v16, w16, *, layer, tt, te):
    t, d = x16.shape
    e = u16.shape[1]
    tt, te = _tile(t, tt, SUBLANES), _tile(e, te, LANES)
    return pl.pallas_call(
        functools.partial(_expert_kernel, tc=_tile(d, 1024, LANES), tt=tt),
        out_shape=jax.ShapeDtypeStruct((t, d), F32),
        grid=(t // tt, e // te),
        in_specs=[_resident_spec((tt, d), lambda i, j: (i, 0)),
                  _layer_spec(layer, (te, d), lambda i, j: (j, 0)),
                  _layer_spec(layer, (te, d), lambda i, j: (j, 0)),
                  pl.BlockSpec((tt, te), lambda i, j: (i, j))],
        out_specs=pl.BlockSpec(memory_space=pl.ANY),
        scratch_shapes=[pltpu.VMEM((tt, d), F32), pltpu.SemaphoreType.DMA],
        compiler_params=_cparams("arbitrary", "arbitrary"),
        name="peer_experts",
    )(x16, u16, v16, w16)


def _prepare_in_proj(w_in, bw):
    fh = bw // HEAD_DIM
    mh = bw // MLSTM_V_DIM
    o_ff = 3 * bw
    o_lx = o_ff + fh
    o_mi = o_lx + 5 * bw
    q_scale = HEAD_DIM ** -0.5 * LOG2E
    w_qv = jnp.concatenate([w_in[:, :, :bw] * q_scale, w_in[:, :, 2 * bw:o_ff]], axis=2)
    w_qv_t = jnp.swapaxes(w_qv, 1, 2).astype(BF16)
    w_main = jnp.concatenate([w_in[:, :, bw:2 * bw], w_in[:, :, o_lx:o_mi]], axis=2).astype(BF16)
    n_gate = fh + 2 * mh
    w_gate = jnp.concatenate([w_in[:, :, o_ff:o_lx], w_in[:, :, o_mi:]], axis=2)
    w_gate = jnp.pad(w_gate, ((0, 0), (0, 0), (0, LANES - n_gate))).astype(BF16)
    return w_main, w_qv_t, w_gate


def _mixer(h16, batch, layer, w_main, w_qv_t, w_gate, fox_f_bias, lru_conv_w, lru_conv_b,
           lru_gate_a_w, lru_gate_a_b, lru_gate_x_w, lru_gate_x_b, lru_lambda, mlstm_conv_w,
           mlstm_conv_b, mlstm_i_bias, mlstm_f_bias, w_branch, w_merge_gate, b_merge_gate, w_out):
    t, d = h16.shape
    bw = w_branch.shape[2]
    fh = bw // HEAD_DIM
    mh = bw // MLSTM_V_DIM
    gate_bias = jnp.pad(jnp.concatenate([fox_f_bias, mlstm_i_bias, mlstm_f_bias]),
                        (0, LANES - (fh + 2 * mh))).reshape(1, LANES)

    proj = _matmul(h16, w_main, BF16, layer=layer, tm=1024, tn=1024, name="in_proj")
    gate_z = _matmul(h16, w_gate, F32, layer=layer, tm=1024, tn=LANES, name="gate_proj")
    gates = _gate_prep(gate_z, gate_bias, batch=batch, fh=fh, mh=mh)
    qv_t = _matmul_t(h16, w_qv_t, BF16, layer=layer, tm=1024, tn=1024, name="qv_proj_t")
    cum = gates.T[:fh]
    cum_rep = jnp.broadcast_to(cum[:, :, None], (fh, t, LANES))
    y_fox = _fox_attention(qv_t, proj, cum_rep, cum.reshape(fh, 1, t), batch=batch, fh=fh, tq=512)
    y_lru = _lru_branch(proj, lru_conv_w, lru_conv_b.reshape(1, bw), lru_gate_a_w.astype(BF16),
                        lru_gate_a_b.reshape(1, bw), lru_gate_x_w.astype(BF16),
                        lru_gate_x_b.reshape(1, bw), lru_lambda.reshape(1, bw),
                        batch=batch, bw=bw, ts=256)
    gates_ct = gates.reshape(t // CHUNK, CHUNK, LANES).transpose(0, 2, 1)
    y_mlstm = _mlstm_branch(proj, gates, gates_ct, mlstm_conv_w, mlstm_conv_b.reshape(1, bw),
                            batch=batch, bw=bw, fh=fh, ts=512)
    merged = _merge(h16, (y_fox, y_lru, y_mlstm), w_merge_gate, b_merge_gate, w_branch,
                    layer=layer, tm=1024, tn=512)
    return _matmul(merged, w_out, F32, layer=layer, tm=1024, tn=1024, name="out_proj")


def _peer(h16, layer, w_query, sub_keys, expert_u, expert_v):
    q16 = _matmul(h16, w_query, BF16, layer=layer, tm=1024, tn=1024, name="peer_query")
    sel_i, sel_j, gates = _peer_route(q16, sub_keys.astype(BF16), tt=512)
    w16 = _peer_gates(sel_i.T, sel_j.T, gates.T, tb=128)
    return _peer_experts(h16, expert_u, expert_v, w16, layer=layer, tt=1024, te=512)


def kernel(x, w_in, fox_f_bias, lru_conv_w, lru_conv_b, lru_gate_a_w, lru_gate_a_b, lru_gate_x_w, lru_gate_x_b, lru_lambda, mlstm_conv_w, mlstm_conv_b, mlstm_i_bias, mlstm_f_bias, w_branch, w_merge_gate, b_merge_gate, w_out, ln1_g, ln1_b, peer_w_query, peer_sub_keys, peer_u, peer_v, ln2_g, ln2_b):
    batch, seq, d = x.shape
    depth = w_in.shape[0]
    alpha = (2.0 * depth) ** 0.25
    w_main, w_qv_t, w_gate = _prepare_in_proj(w_in, w_branch.shape[2])
    w_branch16 = w_branch.astype(BF16)
    w_merge16 = w_merge_gate.astype(BF16)
    b_merge = b_merge_gate.reshape(depth, 3, 1, d)
    w_out16 = w_out.astype(BF16)
    w_query16 = peer_w_query.astype(BF16)
    peer_u16 = peer_u.astype(BF16)
    peer_v16 = peer_v.astype(BF16)
    h = x.reshape(batch * seq, d)
    h16 = h.astype(BF16)
    for l in range(depth):
        mix = _mixer(h16, batch, l, w_main, w_qv_t, w_gate, fox_f_bias[l], lru_conv_w[l], lru_conv_b[l],
                     lru_gate_a_w[l], lru_gate_a_b[l], lru_gate_x_w[l], lru_gate_x_b[l], lru_lambda[l],
                     mlstm_conv_w[l], mlstm_conv_b[l], mlstm_i_bias[l], mlstm_f_bias[l], w_branch16,
                     w_merge16, b_merge, w_out16)
        h, h16 = _post_norm(h, mix, ln1_g[l].reshape(1, d), ln1_b[l].reshape(1, d), alpha=alpha, tr=256)
        ffn = _peer(h16, l, w_query16, peer_sub_keys[l], peer_u16, peer_v16)
        h, h16 = _post_norm(h, ffn, ln2_g[l].reshape(1, d), ln2_b[l].reshape(1, d), alpha=alpha, tr=256)
    return h.reshape(batch, seq, d)
```

```python
import functools
import math

import jax
import jax.numpy as jnp
from jax import lax
from jax.experimental import pallas as pl
from jax.experimental.pallas import tpu as pltpu

F32 = jnp.float32
BF16 = jnp.bfloat16

HEAD_DIM = 128
MLSTM_V_DIM = 256
CHUNK = 64
CONV_WIDTH = 4
LRU_C = 8.0
PEER_KEYS = 128
PEER_HALF = 128
PEER_TOPK = 16
LN_EPS = 1e-5
GATE_TOKEN_UNROLL = 32

V7X_VMEM_BYTES = 64 * 1024 * 1024
VMEM_LIMIT = V7X_VMEM_BYTES - 8 * 1024 * 1024
LANES = 128
SUBLANES = 8

NEG_INF = float("-inf")
LOG2E = math.log2(math.e)

PROJ_FK, PROJ_LX, PROJ_LG, PROJ_MQK, PROJ_MV, PROJ_MO = range(6)


def _tile(dim, pref, align):
    if dim <= pref:
        return dim
    t = pref - pref % align
    while dim % t:
        t -= align
    assert t > 0, (dim, pref, align)
    return t


def _layer_spec(layer, block_shape, index_map):
    return pl.BlockSpec((None,) + tuple(block_shape), lambda *g: (layer,) + tuple(index_map(*g)))


def _resident_spec(block_shape, index_map):
    return pl.BlockSpec(block_shape, index_map, pipeline_mode=pl.Buffered(1))


def _cparams(*sem):
    return pltpu.CompilerParams(dimension_semantics=sem, vmem_limit_bytes=VMEM_LIMIT)


def _log_sigmoid(x):
    return jnp.minimum(x, 0.0) - jnp.log1p(jnp.exp(-jnp.abs(x)))


def _sigmoid(x):
    return 1.0 / (1.0 + jnp.exp(-x))


def _gelu_tanh(x):
    c = math.sqrt(2.0 / math.pi)
    return 0.5 * x * (1.0 + jnp.tanh(c * (x + 0.044715 * (x * x * x))))


def _expm1(z):
    u = jnp.exp(z)
    um1 = u - 1.0
    near = um1 * z / jnp.log(u)
    return jnp.where(jnp.abs(z) > 0.5, um1, jnp.where(u == 1.0, z, near))


def _split3(x):
    hi = x.astype(BF16)
    r1 = x - hi.astype(F32)
    mid = r1.astype(BF16)
    lo = (r1 - mid.astype(F32)).astype(BF16)
    return hi, mid, lo


def _dot(a, b):
    return jnp.dot(a, b, preferred_element_type=F32)


def _dot_nt(a, b):
    return lax.dot_general(a, b, (((1,), (1,)), ((), ())), preferred_element_type=F32)


def _dot_tn(a, b):
    return lax.dot_general(a, b, (((0,), (0,)), ((), ())), preferred_element_type=F32)


def _dot3_left(tri, x):
    hi, mid, lo = _split3(x)
    return _dot(tri, hi) + _dot(tri, mid) + _dot(tri, lo)


def _dot3_right(x, tri):
    hi, mid, lo = _split3(x)
    return _dot(hi, tri) + _dot(mid, tri) + _dot(lo, tri)


def _mm_kernel(a_ref, b_ref, o_ref):
    o_ref[...] = _dot(a_ref[...], b_ref[...]).astype(o_ref.dtype)


def _matmul(a, b, out_dtype, *, layer, tm, tn, name):
    m, k = a.shape
    n = b.shape[2]
    tm, tn = _tile(m, tm, SUBLANES), _tile(n, tn, LANES)
    return pl.pallas_call(
        _mm_kernel,
        out_shape=jax.ShapeDtypeStruct((m, n), out_dtype),
        grid=(m // tm, n // tn),
        in_specs=[pl.BlockSpec((tm, k), lambda i, j: (i, 0)),
                  _layer_spec(layer, (k, tn), lambda i, j: (0, j))],
        out_specs=pl.BlockSpec((tm, tn), lambda i, j: (i, j)),
        compiler_params=_cparams("parallel", "parallel"),
        name=name,
    )(a, b)


def _mm_t_kernel(a_ref, b_ref, o_ref):
    o_ref[...] = lax.dot_general(b_ref[...], a_ref[...], (((0,), (1,)), ((), ())),
                                 preferred_element_type=F32).astype(o_ref.dtype)


def _matmul_t(a, b, out_dtype, *, layer, tm, tn, name):
    m, k = a.shape
    n = b.shape[2]
    tm, tn = _tile(m, tm, SUBLANES), _tile(n, tn, LANES)
    return pl.pallas_call(
        _mm_t_kernel,
        out_shape=jax.ShapeDtypeStruct((n, m), out_dtype),
        grid=(m // tm, n // tn),
        in_specs=[pl.BlockSpec((tm, k), lambda i, j: (i, 0)),
                  _layer_spec(layer, (k, tn), lambda i, j: (0, j))],
        out_specs=pl.BlockSpec((tn, tm), lambda i, j: (j, i)),
        compiler_params=_cparams("parallel", "parallel"),
        name=name,
    )(a, b)


def _gate_kernel(z_ref, b_ref, o_ref, carry_ref, *, fh, mh, ts):
    @pl.when(pl.program_id(1) == 0)
    def _():
        carry_ref[...] = jnp.zeros_like(carry_ref)

    z = z_ref[...] + b_ref[...]
    ls = _log_sigmoid(z)
    lane = lax.broadcasted_iota(jnp.int32, z.shape, 1)
    row = lax.broadcasted_iota(jnp.int32, (ts, ts), 0)
    col = lax.broadcasted_iota(jnp.int32, (ts, ts), 1)
    tri = jnp.where(col <= row, 1.0, 0.0).astype(BF16)
    cum = _dot3_left(tri, ls) + carry_ref[...]
    carry_ref[...] = cum[ts - 1:ts, :]
    o_ref[...] = jnp.where(lane < fh, cum, jnp.where(lane < fh + mh, z, ls))


def _gate_prep(z, bias, *, batch, fh, mh):
    t = z.shape[0]
    s = t // batch
    ts = min(256, s)
    ns = s // ts
    return pl.pallas_call(
        functools.partial(_gate_kernel, fh=fh, mh=mh, ts=ts),
        out_shape=jax.ShapeDtypeStruct((t, LANES), F32),
        grid=(batch, ns),
        in_specs=[pl.BlockSpec((ts, LANES), lambda b, i: (b * ns + i, 0)),
                  pl.BlockSpec((1, LANES), lambda b, i: (0, 0))],
        out_specs=pl.BlockSpec((ts, LANES), lambda b, i: (b * ns + i, 0)),
        scratch_shapes=[pltpu.VMEM((1, LANES), F32)],
        compiler_params=_cparams("parallel", "arbitrary"),
        name="gate_prep",
    )(z, bias)


def _fox_kernel(qt_ref, k_ref, vt_ref, ck_ref, cq_ref, o_ref, m_ref, acc_ref, r_ref, kx_ref, *, tq):
    qi = pl.program_id(2)
    dh = HEAD_DIM

    @pl.when(qi == 0)
    def _():
        hi, mid, lo = _split3(ck_ref[...] * LOG2E)
        lane = lax.broadcasted_iota(jnp.int32, hi.shape, 1)
        zero = jnp.zeros_like(hi)
        kx_ref[:, :dh] = k_ref[...]
        kx_ref[:, dh:] = jnp.where(lane == 0, hi, jnp.where(lane == 1, mid, jnp.where(lane == 2, lo, zero)))

    sub = lax.broadcasted_iota(jnp.int32, (dh, tq), 0)
    qx = jnp.concatenate([qt_ref[...], jnp.where(sub < 3, -1.0, 0.0).astype(BF16)], axis=0)
    cq = cq_ref[...] * LOG2E
    ones = jnp.ones((SUBLANES, tq), BF16)
    m_ref[...] = jnp.full_like(m_ref, NEG_INF)
    acc_ref[...] = jnp.zeros_like(acc_ref)

    def scores(kj, slot):
        start = pl.multiple_of(kj * tq, tq)
        r_ref[slot] = _dot(kx_ref[pl.ds(start, tq), :], qx)

    def update(kj, slot, masked):
        start = pl.multiple_of(kj * tq, tq)
        vt1 = jnp.concatenate([vt_ref[:, pl.ds(start, tq)], ones], axis=0)
        r = r_ref[slot]
        if masked:
            key = lax.broadcasted_iota(jnp.int32, (tq, tq), 0)
            qry = lax.broadcasted_iota(jnp.int32, (tq, tq), 1)
            r = jnp.where(key <= qry, r, NEG_INF)
        m_prev = m_ref[...]
        m_new = jnp.maximum(m_prev, jnp.max(r, axis=0, keepdims=True) + cq)
        p = jnp.exp2(r + (cq - m_new))
        alpha = jnp.exp2(m_prev - m_new)
        acc_ref[...] = alpha * acc_ref[...] + _dot(vt1, p.astype(BF16))
        m_ref[...] = m_new

    scores(0, 0)

    def body(pair, c):
        scores(2 * pair + 1, 1)
        update(2 * pair, 0, False)
        scores(2 * pair + 2, 0)
        update(2 * pair + 1, 1, False)
        return c

    lax.fori_loop(0, qi // 2, body, 0)

    @pl.when(qi % 2 == 1)
    def _():
        scores(qi, 1)
        update(qi - 1, 0, False)
        update(qi, 1, True)

    @pl.when(qi % 2 == 0)
    def _():
        update(qi, 0, True)

    o_ref[...] = (acc_ref[:dh, :] / acc_ref[dh:dh + 1, :]).T.astype(o_ref.dtype)


def _fox_attention(qv_t, proj, cum_rep, cum_row, *, batch, fh, tq):
    t = proj.shape[0]
    s = t // batch
    tq = _tile(s, tq, LANES)
    nq = s // tq
    return pl.pallas_call(
        functools.partial(_fox_kernel, tq=tq),
        out_shape=jax.ShapeDtypeStruct((t, fh * HEAD_DIM), BF16),
        grid=(batch, fh, nq),
        in_specs=[pl.BlockSpec((HEAD_DIM, tq), lambda b, h, i: (h, b * nq + i)),
                  pl.BlockSpec((s, HEAD_DIM), lambda b, h, i: (b, h)),
                  pl.BlockSpec((HEAD_DIM, s), lambda b, h, i: (fh + h, b)),
                  pl.BlockSpec((None, s, LANES), lambda b, h, i: (h, b, 0)),
                  pl.BlockSpec((None, 1, tq), lambda b, h, i: (h, 0, b * nq + i))],
        out_specs=pl.BlockSpec((tq, HEAD_DIM), lambda b, h, i: (b * nq + i, h)),
        scratch_shapes=[pltpu.VMEM((1, tq), F32), pltpu.VMEM((HEAD_DIM + SUBLANES, tq), F32),
                        pltpu.VMEM((2, tq, tq), F32), pltpu.VMEM((s, 2 * HEAD_DIM), BF16)],
        compiler_params=_cparams("parallel", "parallel", "arbitrary"),
        name="fox_attention",
    )(qv_t, proj, qv_t, cum_rep, cum_row)


def _causal_conv(x, ext_ref, w_ref, b_ref, ts, first):
    @pl.when(first)
    def _():
        ext_ref[0:SUBLANES, :] = jnp.zeros((SUBLANES, x.shape[1]), F32)

    ext_ref[SUBLANES:SUBLANES + ts, :] = x
    y = b_ref[...] + w_ref[CONV_WIDTH - 1:CONV_WIDTH, :] * x
    for k in range(CONV_WIDTH - 1):
        off = SUBLANES - (CONV_WIDTH - 1) + k
        y = y + w_ref[k:k + 1, :] * ext_ref[off:off + ts, :]
    ext_ref[0:SUBLANES, :] = ext_ref[ts:ts + SUBLANES, :]
    return y


def _lru_kernel(lx_ref, lg_ref, cw_ref, cb_ref, wa_ref, ba_ref, wx_ref, bx_ref, lam_ref, o_ref,
                ext_ref, a_ref, u_ref, h_ref, *, ts, nblk):
    first = pl.program_id(1) == 0

    @pl.when(first)
    def _():
        h_ref[...] = jnp.zeros_like(h_ref)

    xl = _causal_conv(lx_ref[...].astype(F32), ext_ref, cw_ref, cb_ref, ts, first)
    lam = lam_ref[...]
    nsp = jnp.maximum(-lam, 0.0) + jnp.log1p(jnp.exp(-jnp.abs(lam)))
    for n in range(nblk):
        sl = slice(n * HEAD_DIM, (n + 1) * HEAD_DIM)
        xb = xl[:, sl]
        xb16 = xb.astype(BF16)
        r = _sigmoid(_dot(xb16, wa_ref[n]) + ba_ref[:, sl])
        gi = _sigmoid(_dot(xb16, wx_ref[n]) + bx_ref[:, sl])
        log_a = (-LRU_C) * r * nsp[:, sl]
        a_ref[:, sl] = jnp.exp(log_a)
        u_ref[:, sl] = jnp.sqrt(-_expm1(2.0 * log_a)) * gi * xb

    def group(g, h):
        base = pl.multiple_of(g * SUBLANES, SUBLANES)
        a8 = a_ref[pl.ds(base, SUBLANES), :]
        u8 = u_ref[pl.ds(base, SUBLANES), :]
        rows = []
        for r in range(SUBLANES):
            h = a8[r:r + 1, :] * h + u8[r:r + 1, :]
            rows.append(h)
        u_ref[pl.ds(base, SUBLANES), :] = jnp.concatenate(rows, axis=0)
        return h

    h_ref[...] = lax.fori_loop(0, ts // SUBLANES, group, h_ref[...])
    o_ref[...] = (u_ref[...] * _gelu_tanh(lg_ref[...].astype(F32))).astype(o_ref.dtype)


def _lru_branch(proj, conv_w, conv_b, w_a, b_a, w_x, b_x, lam, *, batch, bw, ts):
    t = proj.shape[0]
    s = t // batch
    ts = _tile(s, ts, CHUNK)
    ns = s // ts
    nblk = bw // HEAD_DIM
    row = lambda b, i: b * ns + i
    full2 = lambda b, i: (0, 0)
    full3 = lambda b, i: (0, 0, 0)
    return pl.pallas_call(
        functools.partial(_lru_kernel, ts=ts, nblk=nblk),
        out_shape=jax.ShapeDtypeStruct((t, bw), BF16),
        grid=(batch, ns),
        in_specs=[pl.BlockSpec((ts, bw), lambda b, i: (row(b, i), PROJ_LX)),
                  pl.BlockSpec((ts, bw), lambda b, i: (row(b, i), PROJ_LG)),
                  pl.BlockSpec((CONV_WIDTH, bw), full2),
                  pl.BlockSpec((1, bw), full2),
                  pl.BlockSpec((nblk, HEAD_DIM, HEAD_DIM), full3),
                  pl.BlockSpec((1, bw), full2),
                  pl.BlockSpec((nblk, HEAD_DIM, HEAD_DIM), full3),
                  pl.BlockSpec((1, bw), full2),
                  pl.BlockSpec((1, bw), full2)],
        out_specs=pl.BlockSpec((ts, bw), lambda b, i: (row(b, i), 0)),
        scratch_shapes=[pltpu.VMEM((ts + SUBLANES, bw), F32), pltpu.VMEM((ts, bw), F32),
                        pltpu.VMEM((ts, bw), F32), pltpu.VMEM((1, bw), F32)],
        compiler_params=_cparams("parallel", "arbitrary"),
        name="rglru_branch",
    )(proj, proj, conv_w, conv_b, w_a, b_a, w_x, b_x, lam)


def _mlstm_kernel(qk_ref, v_ref, og_ref, g_ref, gt_ref, cw_ref, cb_ref, o_ref,
                  ext_ref, qk_s, c_ref, n_ref, m_ref, *, ts, mh, gate_lane):
    first = pl.program_id(1) == 0

    @pl.when(first)
    def _():
        c_ref[...] = jnp.zeros_like(c_ref)
        n_ref[...] = jnp.zeros_like(n_ref)
        m_ref[...] = jnp.zeros_like(m_ref)

    conv = _causal_conv(qk_ref[...].astype(F32), ext_ref, cw_ref, cb_ref, ts, first)
    qk_s[...] = conv * _sigmoid(conv)
    mqk = mh * HEAD_DIM
    kscale = HEAD_DIM ** -0.5

    row = lax.broadcasted_iota(jnp.int32, (CHUNK, CHUNK), 0)
    col = lax.broadcasted_iota(jnp.int32, (CHUNK, CHUNK), 1)
    tril = col <= row
    tri_lo = jnp.where(tril, 1.0, 0.0).astype(BF16)
    tri_up = jnp.where(row <= col, 1.0, 0.0).astype(BF16)

    def chunk(c, carry):
        r0 = pl.multiple_of(c * CHUNK, CHUNK)
        gates = g_ref[pl.ds(r0, CHUNK), :]
        gates_t = gt_ref[c]
        cum_c = _dot3_left(tri_lo, gates)
        cum_r = _dot3_right(gates_t, tri_up)
        results = []
        for h in range(mh):
            li, lf = gate_lane + h, gate_lane + mh + h
            g_col = cum_c[:, lf:lf + 1]
            g_row = cum_r[lf:lf + 1, :]
            ig_col = gates[:, li:li + 1]
            ig_row = gates_t[li:li + 1, :]
            q = qk_s[pl.ds(r0, CHUNK), h * HEAD_DIM:(h + 1) * HEAD_DIM]
            k = qk_s[pl.ds(r0, CHUNK), mqk + h * HEAD_DIM:mqk + (h + 1) * HEAD_DIM] * kscale
            v = v_ref[pl.ds(r0, CHUNK), h * MLSTM_V_DIM:(h + 1) * MLSTM_V_DIM]
            q16 = q.astype(BF16)
            m_prev = m_ref[h]
            c_prev = c_ref[h]
            n_prev = n_ref[h]

            log_d = jnp.where(tril, g_col - g_row + ig_row, NEG_INF)
            m_inter = g_col + m_prev
            m_row = jnp.maximum(m_inter, jnp.max(log_d, axis=1, keepdims=True))
            sc = _dot_nt(q16, k.astype(BF16)) * jnp.exp(log_d - m_row)
            inter = jnp.exp(m_inter - m_row)
            num = _dot(sc.astype(BF16), v) + inter * _dot(q16, c_prev.astype(BF16))
            den = jnp.sum(sc, axis=1, keepdims=True) + inter * jnp.sum(q * n_prev, axis=1, keepdims=True)
            hh = num / jnp.maximum(jnp.abs(den), jnp.exp(-m_row))
            og = og_ref[pl.ds(r0, CHUNK), h * MLSTM_V_DIM:(h + 1) * MLSTM_V_DIM].astype(F32)
            out = (_sigmoid(og) * hh).astype(o_ref.dtype)

            g_last = g_col[CHUNK - 1:CHUNK, :]
            log_w = g_last - g_col + ig_col
            m_new = jnp.maximum(g_last + m_prev, jnp.max(log_w, axis=0, keepdims=True))
            w = jnp.exp(log_w - m_new)
            decay = jnp.exp(g_last + m_prev - m_new)
            wk = w * k
            c_new = decay * c_prev + _dot_tn(wk.astype(BF16), v)
            n_new = decay * n_prev + jnp.sum(wk, axis=0, keepdims=True)
            results.append((out, c_new, n_new, m_new))
        for h, (out, c_new, n_new, m_new) in enumerate(results):
            o_ref[pl.ds(r0, CHUNK), h * MLSTM_V_DIM:(h + 1) * MLSTM_V_DIM] = out
            c_ref[h] = c_new
            n_ref[h] = n_new
            m_ref[h] = m_new
        return carry

    lax.fori_loop(0, ts // CHUNK, chunk, 0)


def _mlstm_branch(proj, gates, gates_t, conv_w, conv_b, *, batch, bw, fh, ts):
    t = proj.shape[0]
    s = t // batch
    ts = _tile(s, ts, CHUNK)
    ns = s // ts
    mh = bw // MLSTM_V_DIM
    row = lambda b, i: b * ns + i
    full2 = lambda b, i: (0, 0)
    kern = functools.partial(_mlstm_kernel, ts=ts, mh=mh, gate_lane=fh)
    return pl.pallas_call(
        kern,
        out_shape=jax.ShapeDtypeStruct((t, bw), BF16),
        grid=(batch, ns),
        in_specs=[pl.BlockSpec((ts, bw), lambda b, i: (row(b, i), PROJ_MQK)),
                  pl.BlockSpec((ts, bw), lambda b, i: (row(b, i), PROJ_MV)),
                  pl.BlockSpec((ts, bw), lambda b, i: (row(b, i), PROJ_MO)),
                  pl.BlockSpec((ts, LANES), lambda b, i: (row(b, i), 0)),
                  pl.BlockSpec((ts // CHUNK, LANES, CHUNK), lambda b, i: (row(b, i), 0, 0)),
                  pl.BlockSpec((CONV_WIDTH, bw), full2),
                  pl.BlockSpec((1, bw), full2)],
        out_specs=pl.BlockSpec((ts, bw), lambda b, i: (row(b, i), 0)),
        scratch_shapes=[pltpu.VMEM((ts + SUBLANES, bw), F32), pltpu.VMEM((ts, bw), F32),
                        pltpu.VMEM((mh, HEAD_DIM, MLSTM_V_DIM), F32),
                        pltpu.VMEM((mh, 1, HEAD_DIM), F32), pltpu.VMEM((mh, 1, 1), F32)],
        compiler_params=_cparams("parallel", "arbitrary"),
        name="mlstm_branch",
    )(proj, proj, proj, gates, gates_t, conv_w, conv_b)


def _merge_kernel(x_ref, y0_ref, y1_ref, y2_ref, wg_ref, bg_ref, wb_ref, o_ref, acc_ref):
    b = pl.program_id(2)

    def contrib(y_ref):
        gate = _sigmoid(_dot(x_ref[...], wg_ref[...]) + bg_ref[...])
        return gate * _dot(y_ref[...], wb_ref[...])

    @pl.when(b == 0)
    def _():
        acc_ref[...] = contrib(y0_ref)

    @pl.when(b == 1)
    def _():
        acc_ref[...] += contrib(y1_ref)

    @pl.when(b == 2)
    def _():
        o_ref[...] = (acc_ref[...] + contrib(y2_ref)).astype(o_ref.dtype)


def _merge(x16, ys, wg, bg, wb, *, layer, tm, tn):
    t, d = x16.shape
    bw = ys[0].shape[1]
    tm, tn = _tile(t, tm, SUBLANES), _tile(d, tn, LANES)
    rows = lambda i, j, b: (i, 0)
    return pl.pallas_call(
        _merge_kernel,
        out_shape=jax.ShapeDtypeStruct((t, d), BF16),
        grid=(t // tm, d // tn, 3),
        in_specs=[_resident_spec((tm, d), rows),
                  _resident_spec((tm, bw), rows), _resident_spec((tm, bw), rows),
                  _resident_spec((tm, bw), rows),
                  _layer_spec(layer, (None, d, tn), lambda i, j, b: (b, 0, j)),
                  _layer_spec(layer, (None, 1, tn), lambda i, j, b: (b, 0, j)),
                  _layer_spec(layer, (None, bw, tn), lambda i, j, b: (b, 0, j))],
        out_specs=pl.BlockSpec((tm, tn), lambda i, j, b: (i, j)),
        scratch_shapes=[pltpu.VMEM((tm, tn), F32)],
        compiler_params=_cparams("parallel", "parallel", "arbitrary"),
        name="merge_branches",
    )(x16, ys[0], ys[1], ys[2], wg, bg, wb)


def _ln_kernel(r_ref, d_ref, g_ref, b_ref, o_ref, o16_ref, *, alpha):
    x = alpha * r_ref[...] + d_ref[...]
    mu = jnp.mean(x, axis=-1, keepdims=True)
    xc = x - mu
    var = jnp.mean(xc * xc, axis=-1, keepdims=True)
    y = xc * lax.rsqrt(var + LN_EPS) * g_ref[...] + b_ref[...]
    o_ref[...] = y
    o16_ref[...] = y.astype(BF16)


def _post_norm(resid, delta, g, b, *, alpha, tr):
    t, d = resid.shape
    tr = _tile(t, tr, SUBLANES)
    rows = lambda i: (i, 0)
    return pl.pallas_call(
        functools.partial(_ln_kernel, alpha=alpha),
        out_shape=(jax.ShapeDtypeStruct((t, d), F32), jax.ShapeDtypeStruct((t, d), BF16)),
        grid=(t // tr,),
        in_specs=[pl.BlockSpec((tr, d), rows), pl.BlockSpec((tr, d), rows),
                  pl.BlockSpec((1, d), lambda i: (0, 0)), pl.BlockSpec((1, d), lambda i: (0, 0))],
        out_specs=(pl.BlockSpec((tr, d), rows), pl.BlockSpec((tr, d), rows)),
        compiler_params=_cparams("parallel"),
        name="post_norm",
    )(resid, delta, g, b)


def _topk_rows(arrays, k):
    n = arrays[0].shape[1]
    kid = lax.broadcasted_iota(jnp.int32, (k, n), 0)
    rids = [lax.broadcasted_iota(jnp.int32, a.shape, 0) for a in arrays]

    def step(i, carry):
        out = []
        for (a, vals, idx), rid in zip(carry, rids):
            m = jnp.max(a, axis=0, keepdims=True)
            first = jnp.min(jnp.where(a == m, rid, a.shape[0]), axis=0, keepdims=True)
            vals = jnp.where(kid == i, m, vals)
            idx = jnp.where(kid == i, first, idx)
            a = jnp.where(rid == first, NEG_INF, a)
            out.append((a, vals, idx))
        return tuple(out)

    init = tuple((a, jnp.zeros((k, n), F32), jnp.zeros((k, n), jnp.int32)) for a in arrays)
    res = lax.fori_loop(0, k, step, init)
    return [(vals, idx) for _, vals, idx in res]


def _pair_candidates(v1, v2):
    k, n = v1.shape
    assert k % SUBLANES == 0
    groups, table = [], []
    neg = jnp.full((SUBLANES, n), NEG_INF, F32)
    sub = lax.broadcasted_iota(jnp.int32, (SUBLANES, n), 0)
    a = 0
    while a < k:
        nb = k // (a + 1)
        if nb > 1:
            for b0 in range(0, nb, SUBLANES):
                g = v1[a:a + 1, :] + v2[b0:b0 + SUBLANES, :]
                groups.append(g if nb - b0 >= SUBLANES else jnp.where(sub < nb - b0, g, neg))
                table.append((a, 0, b0, 1))
            a += 1
        else:
            assert a % SUBLANES == 0
            groups.append(v1[a:a + SUBLANES, :] + v2[0:1, :])
            table.append((a, 1, 0, 0))
            a += SUBLANES
    return jnp.concatenate(groups, axis=0), table


def _pair_of_row(table, pos):
    g = jnp.right_shift(pos, SUBLANES.bit_length() - 1)
    r = jnp.bitwise_and(pos, SUBLANES - 1)
    a = jnp.zeros(pos.shape, jnp.int32)
    b = jnp.zeros(pos.shape, jnp.int32)
    for gi, (a0, da, b0, db) in enumerate(table):
        hit = g == gi
        a = jnp.where(hit, a0 + da * r, a)
        b = jnp.where(hit, b0 + db * r, b)
    return a, b


def _take_rows(table, sel):
    k, n = table.shape
    out = jnp.zeros((k, n), jnp.int32)
    for a in range(k):
        out = jnp.where(sel == a, table[a:a + 1, :], out)
    return out


def _route_kernel(q_ref, keys_ref, i_ref, j_ref, g_ref):
    q = q_ref[...]
    s1 = _dot_nt(keys_ref[0], q[:, :PEER_HALF])
    s2 = _dot_nt(keys_ref[1], q[:, PEER_HALF:])
    (v1, i1), (v2, i2) = _topk_rows([s1, s2], PEER_TOPK)
    cand, table = _pair_candidates(v1, v2)
    ((top, pos),) = _topk_rows([cand], PEER_TOPK)
    e = jnp.exp(top - top[0:1, :])
    g_ref[...] = e / jnp.sum(e, axis=0, keepdims=True)
    a, b = _pair_of_row(table, pos)
    i_ref[...] = _take_rows(i1, a)
    j_ref[...] = _take_rows(i2, b)


def _peer_route(q16, keys16, *, tt):
    t = q16.shape[0]
    ph = q16.shape[1] // (2 * PEER_HALF)
    tt = _tile(t, tt, LANES)
    n = ph * PEER_TOPK
    out = jax.ShapeDtypeStruct((n, t), jnp.int32)
    ospec = pl.BlockSpec((PEER_TOPK, tt), lambda i, h: (h, i))
    return pl.pallas_call(
        _route_kernel,
        out_shape=(out, out, jax.ShapeDtypeStruct((n, t), F32)),
        grid=(t // tt, ph),
        in_specs=[pl.BlockSpec((tt, 2 * PEER_HALF), lambda i, h: (i, h)),
                  pl.BlockSpec((2, PEER_KEYS, PEER_HALF), lambda i, h: (0, 0, 0))],
        out_specs=(ospec, ospec, ospec),
        compiler_params=_cparams("parallel", "parallel"),
        name="peer_route",
    )(q16, keys16)


def _gates_kernel(i_ref, j_ref, g_ref, o_ref, scr_ref, *, tb, pitch):
    n = i_ref.shape[1]
    sub = lax.broadcasted_iota(jnp.int32, (PEER_KEYS, n), 0)

    def token(t, c):
        irow = i_ref[pl.ds(t, 1), :]
        jrow = j_ref[pl.ds(t, 1), :]
        grow = g_ref[pl.ds(t, 1), :]
        a = jnp.where(sub == irow, grow, 0.0).astype(BF16)
        b = jnp.where(sub == jrow, 1.0, 0.0).astype(BF16)
        scr_ref[pl.ds(t, PEER_KEYS, stride=pitch), :] = _dot_nt(a, b)
        return c

    lax.fori_loop(0, tb, token, 0, unroll=GATE_TOKEN_UNROLL)
    for i in range(PEER_KEYS):
        o_ref[:, i * PEER_KEYS:(i + 1) * PEER_KEYS] = (
            scr_ref[i * pitch:i * pitch + tb, :].astype(o_ref.dtype))


def _peer_gates(sel_i, sel_j, gates, *, tb):
    t, n = sel_i.shape
    tb = _tile(t, tb, SUBLANES)
    e = PEER_KEYS * PEER_KEYS
    rows = lambda i: (i, 0)
    pitch = tb + SUBLANES if (tb // SUBLANES) % 2 == 0 else tb
    return pl.pallas_call(
        functools.partial(_gates_kernel, tb=tb, pitch=pitch),
        out_shape=jax.ShapeDtypeStruct((t, e), BF16),
        grid=(t // tb,),
        in_specs=[pl.BlockSpec((tb, n), rows), pl.BlockSpec((tb, n), rows), pl.BlockSpec((tb, n), rows)],
        out_specs=pl.BlockSpec((tb, e), rows),
        scratch_shapes=[pltpu.VMEM((pitch * PEER_KEYS, PEER_KEYS), F32)],
        compiler_params=_cparams("parallel"),
        name="peer_gates",
    )(sel_i, sel_j, gates)


def _expert_kernel(x_ref, u_ref, v_ref, w_ref, o_hbm, acc_ref, sem, *, tc, tt):
    i, j = pl.program_id(0), pl.program_id(1)

    @pl.when(j == 0)
    def _():
        acc_ref[...] = jnp.zeros_like(acc_ref)

    act = _gelu_tanh(_dot_nt(x_ref[...], u_ref[...]))
    p = (w_ref[...].astype(F32) * act).astype(BF16)
    for c in range(0, acc_ref.shape[1], tc):
        acc_ref[:, c:c + tc] += _dot(p, v_ref[:, c:c + tc])

    @pl.when(j == pl.num_programs(1) - 1)
    def _():
        copy = pltpu.make_async_copy(acc_ref, o_hbm.at[pl.ds(i * tt, tt), :], sem)
        copy.start()
        copy.wait()


def _peer_experts(x16, u16, v16, w16, *, layer, tt, te):
    t, d = x16.shape
    e = u16.shape[1]
    tt, te = _tile(t, tt, SUBLANES), _tile(e, te, LANES)
    return pl.pallas_call(
        functools.partial(_expert_kernel, tc=_tile(d, 1024, LANES), tt=tt),
        out_shape=jax.ShapeDtypeStruct((t, d), F32),
        grid=(t // tt, e // te),
        in_specs=[_resident_spec((tt, d), lambda i, j: (i, 0)),
                  _layer_spec(layer, (te, d), lambda i, j: (j, 0)),
                  _layer_spec(layer, (te, d), lambda i, j: (j, 0)),
                  pl.BlockSpec((tt, te), lambda i, j: (i, j))],
        out_specs=pl.BlockSpec(memory_space=pl.ANY),
        scratch_shapes=[pltpu.VMEM((tt, d), F32), pltpu.SemaphoreType.DMA],
        compiler_params=_cparams("arbitrary", "arbitrary"),
        name="peer_experts",
    )(x16, u16, v16, w16)


def _prepare_in_proj(w_in, bw):
    fh = bw // HEAD_DIM
    mh = bw // MLSTM_V_DIM
    o_ff = 3 * bw
    o_lx = o_ff + fh
    o_mi = o_lx + 5 * bw
    q_scale = HEAD_DIM ** -0.5 * LOG2E
    w_qv = jnp.concatenate([w_in[:, :, :bw] * q_scale, w_in[:, :, 2 * bw:o_ff]], axis=2).astype(BF16)
    w_main = jnp.concatenate([w_in[:, :, bw:2 * bw], w_in[:, :, o_lx:o_mi]], axis=2).astype(BF16)
    n_gate = fh + 2 * mh
    w_gate = jnp.concatenate([w_in[:, :, o_ff:o_lx], w_in[:, :, o_mi:]], axis=2)
    w_gate = jnp.pad(w_gate, ((0, 0), (0, 0), (0, LANES - n_gate))).astype(BF16)
    return w_main, w_qv, w_gate


def _mixer(h16, batch, layer, w_main, w_qv, w_gate, fox_f_bias, lru_conv_w, lru_conv_b,
           lru_gate_a_w, lru_gate_a_b, lru_gate_x_w, lru_gate_x_b, lru_lambda, mlstm_conv_w,
           mlstm_conv_b, mlstm_i_bias, mlstm_f_bias, w_branch, w_merge_gate, b_merge_gate, w_out):
    t, d = h16.shape
    bw = w_branch.shape[2]
    fh = bw // HEAD_DIM
    mh = bw // MLSTM_V_DIM
    gate_bias = jnp.pad(jnp.concatenate([fox_f_bias, mlstm_i_bias, mlstm_f_bias]),
                        (0, LANES - (fh + 2 * mh))).reshape(1, LANES)

    proj = _matmul(h16, w_main, BF16, layer=layer, tm=1024, tn=1024, name="in_proj")
    gate_z = _matmul(h16, w_gate, F32, layer=layer, tm=1024, tn=LANES, name="gate_proj")
    gates = _gate_prep(gate_z, gate_bias, batch=batch, fh=fh, mh=mh)
    qv_t = _matmul_t(h16, w_qv, BF16, layer=layer, tm=1024, tn=1024, name="qv_proj_t")
    cum = gates.T[:fh]
    cum_rep = jnp.broadcast_to(cum[:, :, None], (fh, t, LANES))
    y_fox = _fox_attention(qv_t, proj, cum_rep, cum.reshape(fh, 1, t), batch=batch, fh=fh, tq=512)
    y_lru = _lru_branch(proj, lru_conv_w, lru_conv_b.reshape(1, bw), lru_gate_a_w.astype(BF16),
                        lru_gate_a_b.reshape(1, bw), lru_gate_x_w.astype(BF16),
                        lru_gate_x_b.reshape(1, bw), lru_lambda.reshape(1, bw),
                        batch=batch, bw=bw, ts=256)
    gates_ct = gates.reshape(t // CHUNK, CHUNK, LANES).transpose(0, 2, 1)
    y_mlstm = _mlstm_branch(proj, gates, gates_ct, mlstm_conv_w, mlstm_conv_b.reshape(1, bw),
                            batch=batch, bw=bw, fh=fh, ts=512)
    merged = _merge(h16, (y_fox, y_lru, y_mlstm), w_merge_gate, b_merge_gate, w_branch,
                    layer=layer, tm=1024, tn=512)
    return _matmul(merged, w_out, F32, layer=layer, tm=1024, tn=1024, name="out_proj")


def _peer(h16, layer, w_query, sub_keys, expert_u, expert_v):
    q16 = _matmul(h16, w_query, BF16, layer=layer, tm=1024, tn=1024, name="peer_query")
    sel_i, sel_j, gates = _peer_route(q16, sub_keys.astype(BF16), tt=512)
    w16 = _peer_gates(sel_i.T, sel_j.T, gates.T, tb=128)
    return _peer_experts(h16, expert_u, expert_v, w16, layer=layer, tt=1024, te=512)


def kernel(x, w_in, fox_f_bias, lru_conv_w, lru_conv_b, lru_gate_a_w, lru_gate_a_b, lru_gate_x_w, lru_gate_x_b, lru_lambda, mlstm_conv_w, mlstm_conv_b, mlstm_i_bias, mlstm_f_bias, w_branch, w_merge_gate, b_merge_gate, w_out, ln1_g, ln1_b, peer_w_query, peer_sub_keys, peer_u, peer_v, ln2_g, ln2_b):
    batch, seq, d = x.shape
    depth = w_in.shape[0]
    alpha = (2.0 * depth) ** 0.25
    w_main, w_qv, w_gate = _prepare_in_proj(w_in, w_branch.shape[2])
    w_branch16 = w_branch.astype(BF16)
    w_merge16 = w_merge_gate.astype(BF16)
    b_merge = b_merge_gate.reshape(depth, 3, 1, d)
    w_out16 = w_out.astype(BF16)
    w_query16 = peer_w_query.astype(BF16)
    peer_u16 = peer_u.astype(BF16)
    peer_v16 = peer_v.astype(BF16)
    h = x.reshape(batch * seq, d)
    h16 = h.astype(BF16)
    for l in range(depth):
        mix = _mixer(h16, batch, l, w_main, w_qv, w_gate, fox_f_bias[l], lru_conv_w[l], lru_conv_b[l],
                     lru_gate_a_w[l], lru_gate_a_b[l], lru_gate_x_w[l], lru_gate_x_b[l], lru_lambda[l],
                     mlstm_conv_w[l], mlstm_conv_b[l], mlstm_i_bias[l], mlstm_f_bias[l], w_branch16,
                     w_merge16, b_merge, w_out16)
        h, h16 = _post_norm(h, mix, ln1_g[l].reshape(1, d), ln1_b[l].reshape(1, d), alpha=alpha, tr=256)
        ffn = _peer(h16, l, w_query16, peer_sub_keys[l], peer_u16, peer_v16)
        h, h16 = _post_norm(h, ffn, ln2_g[l].reshape(1, d), ln2_b[l].reshape(1, d), alpha=alpha, tr=256)
    return h.reshape(batch, seq, d)
```

```python
import functools
import math

import jax
import jax.numpy as jnp
from jax import lax
from jax.experimental import pallas as pl
from jax.experimental.pallas import tpu as pltpu

F32 = jnp.float32
BF16 = jnp.bfloat16

HEAD_DIM = 128
MLSTM_V_DIM = 256
CHUNK = 64
CONV_WIDTH = 4
LRU_C = 8.0
PEER_KEYS = 128
PEER_HALF = 128
PEER_TOPK = 16
LN_EPS = 1e-5
GATE_TOKEN_UNROLL = 32

V7X_VMEM_BYTES = 64 * 1024 * 1024
VMEM_LIMIT = V7X_VMEM_BYTES - 8 * 1024 * 1024
LANES = 128
SUBLANES = 8

NEG_INF = float("-inf")
LOG2E = math.log2(math.e)

PROJ_FK, PROJ_LX, PROJ_LG, PROJ_MQK, PROJ_MV, PROJ_MO = range(6)


def _tile(dim, pref, align):
    if dim <= pref:
        return dim
    t = pref - pref % align
    while dim % t:
        t -= align
    assert t > 0, (dim, pref, align)
    return t


def _layer_spec(layer, block_shape, index_map):
    return pl.BlockSpec((None,) + tuple(block_shape), lambda *g: (layer,) + tuple(index_map(*g)))


def _resident_spec(block_shape, index_map):
    return pl.BlockSpec(block_shape, index_map, pipeline_mode=pl.Buffered(1))


def _cparams(*sem):
    return pltpu.CompilerParams(dimension_semantics=sem, vmem_limit_bytes=VMEM_LIMIT)


def _log_sigmoid(x):
    return jnp.minimum(x, 0.0) - jnp.log1p(jnp.exp(-jnp.abs(x)))


def _sigmoid(x):
    return 1.0 / (1.0 + jnp.exp(-x))


def _gelu_tanh(x):
    c = math.sqrt(2.0 / math.pi)
    return 0.5 * x * (1.0 + jnp.tanh(c * (x + 0.044715 * (x * x * x))))


def _expm1(z):
    u = jnp.exp(z)
    um1 = u - 1.0
    near = um1 * z / jnp.log(u)
    return jnp.where(jnp.abs(z) > 0.5, um1, jnp.where(u == 1.0, z, near))


def _split3(x):
    hi = x.astype(BF16)
    r1 = x - hi.astype(F32)
    mid = r1.astype(BF16)
    lo = (r1 - mid.astype(F32)).astype(BF16)
    return hi, mid, lo


def _dot(a, b):
    return jnp.dot(a, b, preferred_element_type=F32)


def _dot_nt(a, b):
    return lax.dot_general(a, b, (((1,), (1,)), ((), ())), preferred_element_type=F32)


def _dot_tn(a, b):
    return lax.dot_general(a, b, (((0,), (0,)), ((), ())), preferred_element_type=F32)


def _dot3_left(tri, x):
    hi, mid, lo = _split3(x)
    return _dot(tri, hi) + _dot(tri, mid) + _dot(tri, lo)


def _dot3_right(x, tri):
    hi, mid, lo = _split3(x)
    return _dot(hi, tri) + _dot(mid, tri) + _dot(lo, tri)


def _mm_kernel(a_ref, b_ref, o_ref):
    o_ref[...] = _dot(a_ref[...], b_ref[...]).astype(o_ref.dtype)


def _matmul(a, b, out_dtype, *, layer, tm, tn, name):
    m, k = a.shape
    n = b.shape[2]
    tm, tn = _tile(m, tm, SUBLANES), _tile(n, tn, LANES)
    return pl.pallas_call(
        _mm_kernel,
        out_shape=jax.ShapeDtypeStruct((m, n), out_dtype),
        grid=(m // tm, n // tn),
        in_specs=[pl.BlockSpec((tm, k), lambda i, j: (i, 0)),
                  _layer_spec(layer, (k, tn), lambda i, j: (0, j))],
        out_specs=pl.BlockSpec((tm, tn), lambda i, j: (i, j)),
        compiler_params=_cparams("parallel", "parallel"),
        name=name,
    )(a, b)


def _mm_nt_kernel(a_ref, bt_ref, o_ref):
    o_ref[...] = _dot_nt(a_ref[...], bt_ref[...]).astype(o_ref.dtype)


def _matmul_nt(a, bt, out_dtype, *, layer, tm, tn, name):
    m, k = a.shape
    n = bt.shape[1]
    tm, tn = _tile(m, tm, SUBLANES), _tile(n, tn, LANES)
    return pl.pallas_call(
        _mm_nt_kernel,
        out_shape=jax.ShapeDtypeStruct((m, n), out_dtype),
        grid=(m // tm, n // tn),
        in_specs=[pl.BlockSpec((tm, k), lambda i, j: (i, 0)),
                  _layer_spec(layer, (tn, k), lambda i, j: (j, 0))],
        out_specs=pl.BlockSpec((tm, tn), lambda i, j: (i, j)),
        compiler_params=_cparams("parallel", "parallel"),
        name=name,
    )(a, bt)


def _mm_t_kernel(a_ref, bt_ref, o_ref):
    o_ref[...] = _dot_nt(bt_ref[...], a_ref[...]).astype(o_ref.dtype)


def _matmul_t(a, bt, out_dtype, *, layer, tm, tn, name):
    m, k = a.shape
    n = bt.shape[1]
    tm, tn = _tile(m, tm, SUBLANES), _tile(n, tn, LANES)
    return pl.pallas_call(
        _mm_t_kernel,
        out_shape=jax.ShapeDtypeStruct((n, m), out_dtype),
        grid=(m // tm, n // tn),
        in_specs=[pl.BlockSpec((tm, k), lambda i, j: (i, 0)),
                  _layer_spec(layer, (tn, k), lambda i, j: (j, 0))],
        out_specs=pl.BlockSpec((tn, tm), lambda i, j: (j, i)),
        compiler_params=_cparams("parallel", "parallel"),
        name=name,
    )(a, bt)


def _gate_kernel(z_ref, b_ref, o_ref, carry_ref, *, fh, mh, ts):
    @pl.when(pl.program_id(1) == 0)
    def _():
        carry_ref[...] = jnp.zeros_like(carry_ref)

    z = z_ref[...] + b_ref[...]
    ls = _log_sigmoid(z)
    lane = lax.broadcasted_iota(jnp.int32, z.shape, 1)
    row = lax.broadcasted_iota(jnp.int32, (ts, ts), 0)
    col = lax.broadcasted_iota(jnp.int32, (ts, ts), 1)
    tri = jnp.where(col <= row, 1.0, 0.0).astype(BF16)
    cum = _dot3_left(tri, ls) + carry_ref[...]
    carry_ref[...] = cum[ts - 1:ts, :]
    o_ref[...] = jnp.where(lane < fh, cum, jnp.where(lane < fh + mh, z, ls))


def _gate_prep(z, bias, *, batch, fh, mh):
    t = z.shape[0]
    s = t // batch
    ts = min(256, s)
    ns = s // ts
    return pl.pallas_call(
        functools.partial(_gate_kernel, fh=fh, mh=mh, ts=ts),
        out_shape=jax.ShapeDtypeStruct((t, LANES), F32),
        grid=(batch, ns),
        in_specs=[pl.BlockSpec((ts, LANES), lambda b, i: (b * ns + i, 0)),
                  pl.BlockSpec((1, LANES), lambda b, i: (0, 0))],
        out_specs=pl.BlockSpec((ts, LANES), lambda b, i: (b * ns + i, 0)),
        scratch_shapes=[pltpu.VMEM((1, LANES), F32)],
        compiler_params=_cparams("parallel", "arbitrary"),
        name="gate_prep",
    )(z, bias)


def _fox_kernel(qt_ref, k_ref, vt_ref, ck_ref, cq_ref, o_ref, m_ref, acc_ref, r_ref, kx_ref, *, tq):
    qi = pl.program_id(2)
    dh = HEAD_DIM

    @pl.when(qi == 0)
    def _():
        hi, mid, lo = _split3(ck_ref[...] * LOG2E)
        lane = lax.broadcasted_iota(jnp.int32, hi.shape, 1)
        zero = jnp.zeros_like(hi)
        kx_ref[:, :dh] = k_ref[...]
        kx_ref[:, dh:] = jnp.where(lane == 0, hi, jnp.where(lane == 1, mid, jnp.where(lane == 2, lo, zero)))

    sub = lax.broadcasted_iota(jnp.int32, (dh, tq), 0)
    qx = jnp.concatenate([qt_ref[...], jnp.where(sub < 3, -1.0, 0.0).astype(BF16)], axis=0)
    cq = cq_ref[...] * LOG2E
    ones = jnp.ones((SUBLANES, tq), BF16)
    m_ref[...] = jnp.full_like(m_ref, NEG_INF)
    acc_ref[...] = jnp.zeros_like(acc_ref)

    def scores(kj, slot):
        start = pl.multiple_of(kj * tq, tq)
        r_ref[slot] = _dot(kx_ref[pl.ds(start, tq), :], qx)

    def update(kj, slot, masked):
        start = pl.multiple_of(kj * tq, tq)
        vt1 = jnp.concatenate([vt_ref[:, pl.ds(start, tq)], ones], axis=0)
        r = r_ref[slot]
        if masked:
            key = lax.broadcasted_iota(jnp.int32, (tq, tq), 0)
            qry = lax.broadcasted_iota(jnp.int32, (tq, tq), 1)
            r = jnp.where(key <= qry, r, NEG_INF)
        m_prev = m_ref[...]
        m_new = jnp.maximum(m_prev, jnp.max(r, axis=0, keepdims=True) + cq)
        p = jnp.exp2(r + (cq - m_new))
        alpha = jnp.exp2(m_prev - m_new)
        acc_ref[...] = alpha * acc_ref[...] + _dot(vt1, p.astype(BF16))
        m_ref[...] = m_new

    scores(0, 0)

    def body(pair, c):
        scores(2 * pair + 1, 1)
        update(2 * pair, 0, False)
        scores(2 * pair + 2, 0)
        update(2 * pair + 1, 1, False)
        return c

    lax.fori_loop(0, qi // 2, body, 0)

    @pl.when(qi % 2 == 1)
    def _():
        scores(qi, 1)
        update(qi - 1, 0, False)
        update(qi, 1, True)

    @pl.when(qi % 2 == 0)
    def _():
        update(qi, 0, True)

    o_ref[...] = (acc_ref[:dh, :] / acc_ref[dh:dh + 1, :]).T.astype(o_ref.dtype)


def _fox_attention(qv_t, proj, cum_rep, cum_row, *, batch, fh, tq):
    t = proj.shape[0]
    s = t // batch
    tq = _tile(s, tq, LANES)
    nq = s // tq
    return pl.pallas_call(
        functools.partial(_fox_kernel, tq=tq),
        out_shape=jax.ShapeDtypeStruct((t, fh * HEAD_DIM), BF16),
        grid=(batch, fh, nq),
        in_specs=[pl.BlockSpec((HEAD_DIM, tq), lambda b, h, i: (h, b * nq + i)),
                  pl.BlockSpec((s, HEAD_DIM), lambda b, h, i: (b, h)),
                  pl.BlockSpec((HEAD_DIM, s), lambda b, h, i: (fh + h, b)),
                  pl.BlockSpec((None, s, LANES), lambda b, h, i: (h, b, 0)),
                  pl.BlockSpec((None, 1, tq), lambda b, h, i: (h, 0, b * nq + i))],
        out_specs=pl.BlockSpec((tq, HEAD_DIM), lambda b, h, i: (b * nq + i, h)),
        scratch_shapes=[pltpu.VMEM((1, tq), F32), pltpu.VMEM((HEAD_DIM + SUBLANES, tq), F32),
                        pltpu.VMEM((2, tq, tq), F32), pltpu.VMEM((s, 2 * HEAD_DIM), BF16)],
        compiler_params=_cparams("parallel", "parallel", "arbitrary"),
        name="fox_attention",
    )(qv_t, proj, qv_t, cum_rep, cum_row)


def _causal_conv(x, ext_ref, w_ref, b_ref, ts, first):
    @pl.when(first)
    def _():
        ext_ref[0:SUBLANES, :] = jnp.zeros((SUBLANES, x.shape[1]), F32)

    ext_ref[SUBLANES:SUBLANES + ts, :] = x
    y = b_ref[...] + w_ref[CONV_WIDTH - 1:CONV_WIDTH, :] * x
    for k in range(CONV_WIDTH - 1):
        off = SUBLANES - (CONV_WIDTH - 1) + k
        y = y + w_ref[k:k + 1, :] * ext_ref[off:off + ts, :]
    ext_ref[0:SUBLANES, :] = ext_ref[ts:ts + SUBLANES, :]
    return y


def _lru_kernel(lx_ref, lg_ref, cw_ref, cb_ref, wa_ref, ba_ref, wx_ref, bx_ref, lam_ref, o_ref,
                ext_ref, a_ref, u_ref, h_ref, *, ts, nblk):
    first = pl.program_id(1) == 0

    @pl.when(first)
    def _():
        h_ref[...] = jnp.zeros_like(h_ref)

    xl = _causal_conv(lx_ref[...].astype(F32), ext_ref, cw_ref, cb_ref, ts, first)
    lam = lam_ref[...]
    nsp = jnp.maximum(-lam, 0.0) + jnp.log1p(jnp.exp(-jnp.abs(lam)))
    for n in range(nblk):
        sl = slice(n * HEAD_DIM, (n + 1) * HEAD_DIM)
        xb = xl[:, sl]
        xb16 = xb.astype(BF16)
        r = _sigmoid(_dot(xb16, wa_ref[n]) + ba_ref[:, sl])
        gi = _sigmoid(_dot(xb16, wx_ref[n]) + bx_ref[:, sl])
        log_a = (-LRU_C) * r * nsp[:, sl]
        a_ref[:, sl] = jnp.exp(log_a)
        u_ref[:, sl] = jnp.sqrt(-_expm1(2.0 * log_a)) * gi * xb

    def group(g, h):
        base = pl.multiple_of(g * SUBLANES, SUBLANES)
        a8 = a_ref[pl.ds(base, SUBLANES), :]
        u8 = u_ref[pl.ds(base, SUBLANES), :]
        rows = []
        for r in range(SUBLANES):
            h = a8[r:r + 1, :] * h + u8[r:r + 1, :]
            rows.append(h)
        u_ref[pl.ds(base, SUBLANES), :] = jnp.concatenate(rows, axis=0)
        return h

    h_ref[...] = lax.fori_loop(0, ts // SUBLANES, group, h_ref[...])
    o_ref[...] = (u_ref[...] * _gelu_tanh(lg_ref[...].astype(F32))).astype(o_ref.dtype)


def _lru_branch(proj, conv_w, conv_b, w_a, b_a, w_x, b_x, lam, *, batch, bw, ts):
    t = proj.shape[0]
    s = t // batch
    ts = _tile(s, ts, CHUNK)
    ns = s // ts
    nblk = bw // HEAD_DIM
    row = lambda b, i: b * ns + i
    full2 = lambda b, i: (0, 0)
    full3 = lambda b, i: (0, 0, 0)
    return pl.pallas_call(
        functools.partial(_lru_kernel, ts=ts, nblk=nblk),
        out_shape=jax.ShapeDtypeStruct((t, bw), BF16),
        grid=(batch, ns),
        in_specs=[pl.BlockSpec((ts, bw), lambda b, i: (row(b, i), PROJ_LX)),
                  pl.BlockSpec((ts, bw), lambda b, i: (row(b, i), PROJ_LG)),
                  pl.BlockSpec((CONV_WIDTH, bw), full2),
                  pl.BlockSpec((1, bw), full2),
                  pl.BlockSpec((nblk, HEAD_DIM, HEAD_DIM), full3),
                  pl.BlockSpec((1, bw), full2),
                  pl.BlockSpec((nblk, HEAD_DIM, HEAD_DIM), full3),
                  pl.BlockSpec((1, bw), full2),
                  pl.BlockSpec((1, bw), full2)],
        out_specs=pl.BlockSpec((ts, bw), lambda b, i: (row(b, i), 0)),
        scratch_shapes=[pltpu.VMEM((ts + SUBLANES, bw), F32), pltpu.VMEM((ts, bw), F32),
                        pltpu.VMEM((ts, bw), F32), pltpu.VMEM((1, bw), F32)],
        compiler_params=_cparams("parallel", "arbitrary"),
        name="rglru_branch",
    )(proj, proj, conv_w, conv_b, w_a, b_a, w_x, b_x, lam)


def _mlstm_kernel(qk_ref, v_ref, og_ref, g_ref, gt_ref, cw_ref, cb_ref, o_ref,
                  ext_ref, qk_s, c_ref, n_ref, m_ref, *, ts, mh, gate_lane):
    first = pl.program_id(1) == 0

    @pl.when(first)
    def _():
        c_ref[...] = jnp.zeros_like(c_ref)
        n_ref[...] = jnp.zeros_like(n_ref)
        m_ref[...] = jnp.zeros_like(m_ref)

    conv = _causal_conv(qk_ref[...].astype(F32), ext_ref, cw_ref, cb_ref, ts, first)
    qk_s[...] = conv * _sigmoid(conv)
    mqk = mh * HEAD_DIM
    kscale = HEAD_DIM ** -0.5

    row = lax.broadcasted_iota(jnp.int32, (CHUNK, CHUNK), 0)
    col = lax.broadcasted_iota(jnp.int32, (CHUNK, CHUNK), 1)
    tril = col <= row
    tri_lo = jnp.where(tril, 1.0, 0.0).astype(BF16)
    tri_up = jnp.where(row <= col, 1.0, 0.0).astype(BF16)

    def chunk(c, carry):
        r0 = pl.multiple_of(c * CHUNK, CHUNK)
        gates = g_ref[pl.ds(r0, CHUNK), :]
        gates_t = gt_ref[c]
        cum_c = _dot3_left(tri_lo, gates)
        cum_r = _dot3_right(gates_t, tri_up)
        results = []
        for h in range(mh):
            li, lf = gate_lane + h, gate_lane + mh + h
            g_col = cum_c[:, lf:lf + 1]
            g_row = cum_r[lf:lf + 1, :]
            ig_col = gates[:, li:li + 1]
            ig_row = gates_t[li:li + 1, :]
            q = qk_s[pl.ds(r0, CHUNK), h * HEAD_DIM:(h + 1) * HEAD_DIM]
            k = qk_s[pl.ds(r0, CHUNK), mqk + h * HEAD_DIM:mqk + (h + 1) * HEAD_DIM] * kscale
            v = v_ref[pl.ds(r0, CHUNK), h * MLSTM_V_DIM:(h + 1) * MLSTM_V_DIM]
            q16 = q.astype(BF16)
            m_prev = m_ref[h]
            c_prev = c_ref[h]
            n_prev = n_ref[h]

            log_d = jnp.where(tril, g_col - g_row + ig_row, NEG_INF)
            m_inter = g_col + m_prev
            m_row = jnp.maximum(m_inter, jnp.max(log_d, axis=1, keepdims=True))
            sc = _dot_nt(q16, k.astype(BF16)) * jnp.exp(log_d - m_row)
            inter = jnp.exp(m_inter - m_row)
            num = _dot(sc.astype(BF16), v) + inter * _dot(q16, c_prev.astype(BF16))
            den = jnp.sum(sc, axis=1, keepdims=True) + inter * jnp.sum(q * n_prev, axis=1, keepdims=True)
            hh = num / jnp.maximum(jnp.abs(den), jnp.exp(-m_row))
            og = og_ref[pl.ds(r0, CHUNK), h * MLSTM_V_DIM:(h + 1) * MLSTM_V_DIM].astype(F32)
            out = (_sigmoid(og) * hh).astype(o_ref.dtype)

            g_last = g_col[CHUNK - 1:CHUNK, :]
            log_w = g_last - g_col + ig_col
            m_new = jnp.maximum(g_last + m_prev, jnp.max(log_w, axis=0, keepdims=True))
            w = jnp.exp(log_w - m_new)
            decay = jnp.exp(g_last + m_prev - m_new)
            wk = w * k
            c_new = decay * c_prev + _dot_tn(wk.astype(BF16), v)
            n_new = decay * n_prev + jnp.sum(wk, axis=0, keepdims=True)
            results.append((out, c_new, n_new, m_new))
        for h, (out, c_new, n_new, m_new) in enumerate(results):
            o_ref[pl.ds(r0, CHUNK), h * MLSTM_V_DIM:(h + 1) * MLSTM_V_DIM] = out
            c_ref[h] = c_new
            n_ref[h] = n_new
            m_ref[h] = m_new
        return carry

    lax.fori_loop(0, ts // CHUNK, chunk, 0)


def _mlstm_branch(proj, gates, gates_t, conv_w, conv_b, *, batch, bw, fh, ts):
    t = proj.shape[0]
    s = t // batch
    ts = _tile(s, ts, CHUNK)
    ns = s // ts
    mh = bw // MLSTM_V_DIM
    row = lambda b, i: b * ns + i
    full2 = lambda b, i: (0, 0)
    kern = functools.partial(_mlstm_kernel, ts=ts, mh=mh, gate_lane=fh)
    return pl.pallas_call(
        kern,
        out_shape=jax.ShapeDtypeStruct((t, bw), BF16),
        grid=(batch, ns),
        in_specs=[pl.BlockSpec((ts, bw), lambda b, i: (row(b, i), PROJ_MQK)),
                  pl.BlockSpec((ts, bw), lambda b, i: (row(b, i), PROJ_MV)),
                  pl.BlockSpec((ts, bw), lambda b, i: (row(b, i), PROJ_MO)),
                  pl.BlockSpec((ts, LANES), lambda b, i: (row(b, i), 0)),
                  pl.BlockSpec((ts // CHUNK, LANES, CHUNK), lambda b, i: (row(b, i), 0, 0)),
                  pl.BlockSpec((CONV_WIDTH, bw), full2),
                  pl.BlockSpec((1, bw), full2)],
        out_specs=pl.BlockSpec((ts, bw), lambda b, i: (row(b, i), 0)),
        scratch_shapes=[pltpu.VMEM((ts + SUBLANES, bw), F32), pltpu.VMEM((ts, bw), F32),
                        pltpu.VMEM((mh, HEAD_DIM, MLSTM_V_DIM), F32),
                        pltpu.VMEM((mh, 1, HEAD_DIM), F32), pltpu.VMEM((mh, 1, 1), F32)],
        compiler_params=_cparams("parallel", "arbitrary"),
        name="mlstm_branch",
    )(proj, proj, proj, gates, gates_t, conv_w, conv_b)


def _merge_kernel(x_ref, y0_ref, y1_ref, y2_ref, wg_ref, bg_ref, wb_ref, o_ref, acc_ref):
    b = pl.program_id(2)

    def contrib(y_ref):
        gate = _sigmoid(_dot(x_ref[...], wg_ref[...]) + bg_ref[...])
        return gate * _dot(y_ref[...], wb_ref[...])

    @pl.when(b == 0)
    def _():
        acc_ref[...] = contrib(y0_ref)

    @pl.when(b == 1)
    def _():
        acc_ref[...] += contrib(y1_ref)

    @pl.when(b == 2)
    def _():
        o_ref[...] = (acc_ref[...] + contrib(y2_ref)).astype(o_ref.dtype)


def _merge(x16, ys, wg, bg, wb, *, layer, tm, tn):
    t, d = x16.shape
    bw = ys[0].shape[1]
    tm, tn = _tile(t, tm, SUBLANES), _tile(d, tn, LANES)
    rows = lambda i, j, b: (i, 0)
    return pl.pallas_call(
        _merge_kernel,
        out_shape=jax.ShapeDtypeStruct((t, d), BF16),
        grid=(t // tm, d // tn, 3),
        in_specs=[_resident_spec((tm, d), rows),
                  _resident_spec((tm, bw), rows), _resident_spec((tm, bw), rows),
                  _resident_spec((tm, bw), rows),
                  _layer_spec(layer, (None, d, tn), lambda i, j, b: (b, 0, j)),
                  _layer_spec(layer, (None, 1, tn), lambda i, j, b: (b, 0, j)),
                  _layer_spec(layer, (None, bw, tn), lambda i, j, b: (b, 0, j))],
        out_specs=pl.BlockSpec((tm, tn), lambda i, j, b: (i, j)),
        scratch_shapes=[pltpu.VMEM((tm, tn), F32)],
        compiler_params=_cparams("parallel", "parallel", "arbitrary"),
        name="merge_branches",
    )(x16, ys[0], ys[1], ys[2], wg, bg, wb)


def _ln_kernel(r_ref, d_ref, g_ref, b_ref, o_ref, o16_ref, *, alpha):
    x = alpha * r_ref[...] + d_ref[...]
    mu = jnp.mean(x, axis=-1, keepdims=True)
    xc = x - mu
    var = jnp.mean(xc * xc, axis=-1, keepdims=True)
    y = xc * lax.rsqrt(var + LN_EPS) * g_ref[...] + b_ref[...]
    o_ref[...] = y
    o16_ref[...] = y.astype(BF16)


def _post_norm(resid, delta, g, b, *, alpha, tr):
    t, d = resid.shape
    tr = _tile(t, tr, SUBLANES)
    rows = lambda i: (i, 0)
    return pl.pallas_call(
        functools.partial(_ln_kernel, alpha=alpha),
        out_shape=(jax.ShapeDtypeStruct((t, d), F32), jax.ShapeDtypeStruct((t, d), BF16)),
        grid=(t // tr,),
        in_specs=[pl.BlockSpec((tr, d), rows), pl.BlockSpec((tr, d), rows),
                  pl.BlockSpec((1, d), lambda i: (0, 0)), pl.BlockSpec((1, d), lambda i: (0, 0))],
        out_specs=(pl.BlockSpec((tr, d), rows), pl.BlockSpec((tr, d), rows)),
        compiler_params=_cparams("parallel"),
        name="post_norm",
    )(resid, delta, g, b)


def _topk_rows(arrays, k):
    n = arrays[0].shape[1]
    kid = lax.broadcasted_iota(jnp.int32, (k, n), 0)
    rids = [lax.broadcasted_iota(jnp.int32, a.shape, 0) for a in arrays]

    def step(i, carry):
        out = []
        for (a, vals, idx), rid in zip(carry, rids):
            m = jnp.max(a, axis=0, keepdims=True)
            first = jnp.min(jnp.where(a == m, rid, a.shape[0]), axis=0, keepdims=True)
            vals = jnp.where(kid == i, m, vals)
            idx = jnp.where(kid == i, first, idx)
            a = jnp.where(rid == first, NEG_INF, a)
            out.append((a, vals, idx))
        return tuple(out)

    init = tuple((a, jnp.zeros((k, n), F32), jnp.zeros((k, n), jnp.int32)) for a in arrays)
    res = lax.fori_loop(0, k, step, init)
    return [(vals, idx) for _, vals, idx in res]


def _pair_candidates(v1, v2):
    k, n = v1.shape
    assert k % SUBLANES == 0
    groups, table = [], []
    neg = jnp.full((SUBLANES, n), NEG_INF, F32)
    sub = lax.broadcasted_iota(jnp.int32, (SUBLANES, n), 0)
    a = 0
    while a < k:
        nb = k // (a + 1)
        if nb > 1:
            for b0 in range(0, nb, SUBLANES):
                g = v1[a:a + 1, :] + v2[b0:b0 + SUBLANES, :]
                groups.append(g if nb - b0 >= SUBLANES else jnp.where(sub < nb - b0, g, neg))
                table.append((a, 0, b0, 1))
            a += 1
        else:
            assert a % SUBLANES == 0
            groups.append(v1[a:a + SUBLANES, :] + v2[0:1, :])
            table.append((a, 1, 0, 0))
            a += SUBLANES
    return jnp.concatenate(groups, axis=0), table


def _pair_of_row(table, pos):
    g = jnp.right_shift(pos, SUBLANES.bit_length() - 1)
    r = jnp.bitwise_and(pos, SUBLANES - 1)
    a = jnp.zeros(pos.shape, jnp.int32)
    b = jnp.zeros(pos.shape, jnp.int32)
    for gi, (a0, da, b0, db) in enumerate(table):
        hit = g == gi
        a = jnp.where(hit, a0 + da * r, a)
        b = jnp.where(hit, b0 + db * r, b)
    return a, b


def _take_rows(table, sel):
    k, n = table.shape
    out = jnp.zeros((k, n), jnp.int32)
    for a in range(k):
        out = jnp.where(sel == a, table[a:a + 1, :], out)
    return out


def _route_kernel(q_ref, keys_ref, i_ref, j_ref, g_ref):
    q = q_ref[...]
    s1 = _dot_nt(keys_ref[0], q[:, :PEER_HALF])
    s2 = _dot_nt(keys_ref[1], q[:, PEER_HALF:])
    (v1, i1), (v2, i2) = _topk_rows([s1, s2], PEER_TOPK)
    cand, table = _pair_candidates(v1, v2)
    ((top, pos),) = _topk_rows([cand], PEER_TOPK)
    e = jnp.exp(top - top[0:1, :])
    g_ref[...] = e / jnp.sum(e, axis=0, keepdims=True)
    a, b = _pair_of_row(table, pos)
    i_ref[...] = _take_rows(i1, a)
    j_ref[...] = _take_rows(i2, b)


def _peer_route(q16, keys16, *, tt):
    t = q16.shape[0]
    ph = q16.shape[1] // (2 * PEER_HALF)
    tt = _tile(t, tt, LANES)
    n = ph * PEER_TOPK
    out = jax.ShapeDtypeStruct((n, t), jnp.int32)
    ospec = pl.BlockSpec((PEER_TOPK, tt), lambda i, h: (h, i))
    return pl.pallas_call(
        _route_kernel,
        out_shape=(out, out, jax.ShapeDtypeStruct((n, t), F32)),
        grid=(t // tt, ph),
        in_specs=[pl.BlockSpec((tt, 2 * PEER_HALF), lambda i, h: (i, h)),
                  pl.BlockSpec((2, PEER_KEYS, PEER_HALF), lambda i, h: (0, 0, 0))],
        out_specs=(ospec, ospec, ospec),
        compiler_params=_cparams("parallel", "parallel"),
        name="peer_route",
    )(q16, keys16)


def _gates_kernel(i_ref, j_ref, g_ref, o_ref, scr_ref, *, tb, pitch):
    n = i_ref.shape[1]
    sub = lax.broadcasted_iota(jnp.int32, (PEER_KEYS, n), 0)

    def token(t, c):
        irow = i_ref[pl.ds(t, 1), :]
        jrow = j_ref[pl.ds(t, 1), :]
        grow = g_ref[pl.ds(t, 1), :]
        a = jnp.where(sub == irow, grow, 0.0).astype(BF16)
        b = jnp.where(sub == jrow, 1.0, 0.0).astype(BF16)
        scr_ref[pl.ds(t, PEER_KEYS, stride=pitch), :] = _dot_nt(a, b)
        return c

    lax.fori_loop(0, tb, token, 0, unroll=GATE_TOKEN_UNROLL)
    for i in range(PEER_KEYS):
        o_ref[:, i * PEER_KEYS:(i + 1) * PEER_KEYS] = (
            scr_ref[i * pitch:i * pitch + tb, :].astype(o_ref.dtype))


def _peer_gates(sel_i, sel_j, gates, *, tb):
    t, n = sel_i.shape
    tb = _tile(t, tb, SUBLANES)
    e = PEER_KEYS * PEER_KEYS
    rows = lambda i: (i, 0)
    pitch = tb + SUBLANES if (tb // SUBLANES) % 2 == 0 else tb
    return pl.pallas_call(
        functools.partial(_gates_kernel, tb=tb, pitch=pitch),
        out_shape=jax.ShapeDtypeStruct((t, e), BF16),
        grid=(t // tb,),
        in_specs=[pl.BlockSpec((tb, n), rows), pl.BlockSpec((tb, n), rows), pl.BlockSpec((tb, n), rows)],
        out_specs=pl.BlockSpec((tb, e), rows),
        scratch_shapes=[pltpu.VMEM((pitch * PEER_KEYS, PEER_KEYS), F32)],
        compiler_params=_cparams("parallel"),
        name="peer_gates",
    )(sel_i, sel_j, gates)


def _expert_kernel(x_ref, u_ref, v_ref, w_ref, o_hbm, acc_ref, sem, *, tc, tt):
    i, j = pl.program_id(0), pl.program_id(1)

    @pl.when(j == 0)
    def _():
        acc_ref[...] = jnp.zeros_like(acc_ref)

    act = _gelu_tanh(_dot_nt(x_ref[...], u_ref[...]))
    p = (w_ref[...].astype(F32) * act).astype(BF16)
    for c in range(0, acc_ref.shape[1], tc):
        acc_ref[:, c:c + tc] += _dot(p, v_ref[:, c:c + tc])

    @pl.when(j == pl.num_programs(1) - 1)
    def _():
        copy = pltpu.make_async_copy(acc_ref, o_hbm.at[pl.ds(i * tt, tt), :], sem)
        copy.start()
        copy.wait()


def _peer_experts(x16, u16, v16, w16, *, layer, tt, te):
    t, d = x16.shape
    e = u16.shape[1]
    tt, te = _tile(t, tt, SUBLANES), _tile(e, te, LANES)
    return pl.pallas_call(
        functools.partial(_expert_kernel, tc=_tile(d, 1024, LANES), tt=tt),
        out_shape=jax.ShapeDtypeStruct((t, d), F32),
        grid=(t // tt, e // te),
        in_specs=[_resident_spec((tt, d), lambda i, j: (i, 0)),
                  _layer_spec(layer, (te, d), lambda i, j: (j, 0)),
                  _layer_spec(layer, (te, d), lambda i, j: (j, 0)),
                  pl.BlockSpec((tt, te), lambda i, j: (i, j))],
        out_specs=pl.BlockSpec(memory_space=pl.ANY),
        scratch_shapes=[pltpu.VMEM((tt, d), F32), pltpu.SemaphoreType.DMA],
        compiler_params=_cparams("arbitrary", "arbitrary"),
        name="peer_experts",
    )(x16, u16, v16, w16)


def _prepare_in_proj(w_in, bw):
    fh = bw // HEAD_DIM
    mh = bw // MLSTM_V_DIM
    o_ff = 3 * bw
    o_lx = o_ff + fh
    o_mi = o_lx + 5 * bw
    wt = jnp.swapaxes(w_in, 1, 2)
    q_scale = HEAD_DIM ** -0.5 * LOG2E
    w_qv = jnp.concatenate([wt[:, :bw] * q_scale, wt[:, 2 * bw:o_ff]], axis=1).astype(BF16)
    w_main = jnp.concatenate([wt[:, bw:2 * bw], wt[:, o_lx:o_mi]], axis=1).astype(BF16)
    n_gate = fh + 2 * mh
    w_gate = jnp.concatenate([wt[:, o_ff:o_lx], wt[:, o_mi:]], axis=1)
    w_gate = jnp.pad(w_gate, ((0, 0), (0, LANES - n_gate), (0, 0))).astype(BF16)
    return w_main, w_qv, w_gate


def _mixer(h16, batch, layer, w_main, w_qv, w_gate, fox_f_bias, lru_conv_w, lru_conv_b,
           lru_gate_a_w, lru_gate_a_b, lru_gate_x_w, lru_gate_x_b, lru_lambda, mlstm_conv_w,
           mlstm_conv_b, mlstm_i_bias, mlstm_f_bias, w_branch, w_merge_gate, b_merge_gate, w_out):
    t, d = h16.shape
    bw = w_branch.shape[2]
    fh = bw // HEAD_DIM
    mh = bw // MLSTM_V_DIM
    gate_bias = jnp.pad(jnp.concatenate([fox_f_bias, mlstm_i_bias, mlstm_f_bias]),
                        (0, LANES - (fh + 2 * mh))).reshape(1, LANES)

    proj = _matmul_nt(h16, w_main, BF16, layer=layer, tm=1024, tn=1024, name="in_proj")
    gate_z = _matmul_nt(h16, w_gate, F32, layer=layer, tm=1024, tn=LANES, name="gate_proj")
    gates = _gate_prep(gate_z, gate_bias, batch=batch, fh=fh, mh=mh)
    qv_t = _matmul_t(h16, w_qv, BF16, layer=layer, tm=1024, tn=1024, name="qv_proj_t")
    cum = gates.T[:fh]
    cum_rep = jnp.broadcast_to(cum[:, :, None], (fh, t, LANES))
    y_fox = _fox_attention(qv_t, proj, cum_rep, cum.reshape(fh, 1, t), batch=batch, fh=fh, tq=512)
    y_lru = _lru_branch(proj, lru_conv_w, lru_conv_b.reshape(1, bw), lru_gate_a_w.astype(BF16),
                        lru_gate_a_b.reshape(1, bw), lru_gate_x_w.astype(BF16),
                        lru_gate_x_b.reshape(1, bw), lru_lambda.reshape(1, bw),
                        batch=batch, bw=bw, ts=256)
    gates_ct = gates.reshape(t // CHUNK, CHUNK, LANES).transpose(0, 2, 1)
    y_mlstm = _mlstm_branch(proj, gates, gates_ct, mlstm_conv_w, mlstm_conv_b.reshape(1, bw),
                            batch=batch, bw=bw, fh=fh, ts=512)
    merged = _merge(h16, (y_fox, y_lru, y_mlstm), w_merge_gate, b_merge_gate, w_branch,
                    layer=layer, tm=1024, tn=512)
    return _matmul(merged, w_out, F32, layer=layer, tm=1024, tn=1024, name="out_proj")


def _peer(h16, layer, w_query, sub_keys, expert_u, expert_v):
    q16 = _matmul(h16, w_query, BF16, layer=layer, tm=1024, tn=1024, name="peer_query")
    sel_i, sel_j, gates = _peer_route(q16, sub_keys.astype(BF16), tt=512)
    w16 = _peer_gates(sel_i.T, sel_j.T, gates.T, tb=128)
    return _peer_experts(h16, expert_u, expert_v, w16, layer=layer, tt=1024, te=512)


def kernel(x, w_in, fox_f_bias, lru_conv_w, lru_conv_b, lru_gate_a_w, lru_gate_a_b, lru_gate_x_w, lru_gate_x_b, lru_lambda, mlstm_conv_w, mlstm_conv_b, mlstm_i_bias, mlstm_f_bias, w_branch, w_merge_gate, b_merge_gate, w_out, ln1_g, ln1_b, peer_w_query, peer_sub_keys, peer_u, peer_v, ln2_g, ln2_b):
    batch, seq, d = x.shape
    depth = w_in.shape[0]
    alpha = (2.0 * depth) ** 0.25
    w_main, w_qv, w_gate = _prepare_in_proj(w_in, w_branch.shape[2])
    w_branch16 = w_branch.astype(BF16)
    w_merge16 = w_merge_gate.astype(BF16)
    b_merge = b_merge_gate.reshape(depth, 3, 1, d)
    w_out16 = w_out.astype(BF16)
    w_query16 = peer_w_query.astype(BF16)
    peer_u16 = peer_u.astype(BF16)
    peer_v16 = peer_v.astype(BF16)
    h = x.reshape(batch * seq, d)
    h16 = h.astype(BF16)
    for l in range(depth):
        mix = _mixer(h16, batch, l, w_main, w_qv, w_gate, fox_f_bias[l], lru_conv_w[l], lru_conv_b[l],
                     lru_gate_a_w[l], lru_gate_a_b[l], lru_gate_x_w[l], lru_gate_x_b[l], lru_lambda[l],
                     mlstm_conv_w[l], mlstm_conv_b[l], mlstm_i_bias[l], mlstm_f_bias[l], w_branch16,
                     w_merge16, b_merge, w_out16)
        h, h16 = _post_norm(h, mix, ln1_g[l].reshape(1, d), ln1_b[l].reshape(1, d), alpha=alpha, tr=256)
        ffn = _peer(h16, l, w_query16, peer_sub_keys[l], peer_u16, peer_v16)
        h, h16 = _post_norm(h, ffn, ln2_g[l].reshape(1, d), ln2_b[l].reshape(1, d), alpha=alpha, tr=256)
    return h.reshape(batch, seq, d)
```
